```python
import math
import jax
import jax.numpy as jnp
from jax import lax
import numpy as np


D_MODEL = 2048
BATCH = 4
SEQ = 8192
DEPTH = 2

HEAD_DIM = 128
SB_WIDTH = 3 * D_MODEL // 4
SB_HEADS = SB_WIDTH // HEAD_DIM
SB_QBLOCK = 128
SSM_WIDTH = D_MODEL // 4
SSM_GROUP = 16
SSM_GROUPS = SSM_WIDTH // SSM_GROUP
SSM_STATE = 64
EVEN_MIX = SB_WIDTH + SSM_WIDTH
EVEN_IN = 4 * SB_WIDTH + 2 * SSM_WIDTH
MOBA_WIDTH = D_MODEL
MOBA_HEADS = MOBA_WIDTH // HEAD_DIM
MOBA_BLOCK = 256
MOBA_TOPK = 3
MOBA_QCHUNK = 16
ODD_IN = 4 * MOBA_WIDTH
N_EVEN = (DEPTH + 1) // 2
N_ODD = DEPTH // 2
DN_ALPHA = (2 * DEPTH) ** 0.25
DN_BETA = (8 * DEPTH) ** -0.25
LN_EPS = 1e-5
NEG = -1e30

kernel_name = "hybrid_stickbreak_s5_moba_deepnorm"


def _layer_norm(x, g, b):
    xf = x.astype(jnp.float32)
    mu = jnp.mean(xf, axis=-1, keepdims=True)
    var = jnp.mean(jnp.square(xf - mu), axis=-1, keepdims=True)
    y = (xf - mu) * lax.rsqrt(var + LN_EPS) * g.astype(jnp.float32) + b.astype(jnp.float32)
    return y.astype(x.dtype)


def _split_heads(t, n_heads):
    b, s, _ = t.shape
    return t.reshape(b, s, n_heads, HEAD_DIM).transpose(0, 2, 1, 3)


def _merge_heads(t):
    b, h, s, d = t.shape
    return t.transpose(0, 2, 1, 3).reshape(b, s, h * d)


def stick_breaking_attention(q, k, v):
    _, _, s, d = q.shape
    scale = d ** -0.5
    outs = []
    for blk in range(s // SB_QBLOCK):
        start = blk * SB_QBLOCK
        end = start + SB_QBLOCK
        z = jnp.einsum('bhqd,bhkd->bhqk', q[:, :, start:end], k[:, :, :end]).astype(jnp.float32) * scale
        t_pos = start + jnp.arange(SB_QBLOCK)
        s_pos = jnp.arange(end)
        past = s_pos[None, :] < t_pos[:, None]
        log_beta = jax.nn.log_sigmoid(z)
        log_keep = jnp.where(past, log_beta - z, 0.0)
        later = lax.cumsum(log_keep, axis=3, reverse=True) - log_keep
        w = jnp.where(past, jnp.exp(log_beta + later), 0.0)
        outs.append(jnp.einsum('bhqk,bhkd->bhqd', w.astype(v.dtype), v[:, :, :end]))
    return jnp.concatenate(outs, axis=2)


def _linear_recurrence(left, right):
    a_l, b_l = left
    a_r, b_r = right
    return a_r * a_l, a_r * b_l + b_r


def s5_ssm(u, a_re, a_im, log_dt, b_re, b_im, c_re, c_im, d_skip, w_glu):
    bsz, s, _ = u.shape
    uf = u.astype(jnp.float32).reshape(bsz, s, SSM_GROUPS, SSM_GROUP)
    lam = lax.complex(a_re.astype(jnp.float32), a_im.astype(jnp.float32))
    dt = jnp.exp(log_dt.astype(jnp.float32))[:, None]
    a_bar = jnp.exp(lam * dt)
    b_c = lax.complex(b_re.astype(jnp.float32), b_im.astype(jnp.float32))
    b_bar = ((a_bar - 1.0) / lam)[..., None] * b_c
    bu = jnp.einsum('bsgi,gpi->bsgp', uf.astype(jnp.complex64), b_bar)
    a_seq = jnp.broadcast_to(a_bar, bu.shape)
    _, h = lax.associative_scan(_linear_recurrence, (a_seq, bu), axis=1)
    c_c = lax.complex(c_re.astype(jnp.float32), c_im.astype(jnp.float32))
    y = jnp.real(jnp.einsum('bsgp,gip->bsgi', h, c_c))
    y = y + d_skip.astype(jnp.float32).reshape(SSM_GROUPS, SSM_GROUP) * uf
    y = jax.nn.gelu(y.reshape(bsz, s, SSM_WIDTH))
    y = y * jax.nn.sigmoid(y @ w_glu.astype(jnp.float32))
    return y.astype(u.dtype)


def moba_attention(q, k, v):
    bsz, h, s, d = q.shape
    nb = -(-s // MOBA_BLOCK)
    pad = nb * MOBA_BLOCK - s
    kb = jnp.pad(k, ((0, 0), (0, 0), (0, pad), (0, 0))).reshape(bsz, h, nb, MOBA_BLOCK, d)
    vb = jnp.pad(v, ((0, 0), (0, 0), (0, pad), (0, 0))).reshape(bsz, h, nb, MOBA_BLOCK, d)
    k_mean = jnp.mean(kb.astype(jnp.float32), axis=3)
    gate = jnp.einsum('bhsd,bhnd->bhsn', q.astype(jnp.float32), k_mean)
    q_block = jnp.arange(s) // MOBA_BLOCK
    fully_past = jnp.arange(nb)[None, :] < q_block[:, None]
    gate = jnp.where(fully_past, gate, NEG)
    n_sel = min(MOBA_TOPK, nb)
    top_val, top_idx = lax.top_k(gate, n_sel)
    sel_valid = top_val > 0.5 * NEG
    n_chunks = s // MOBA_QCHUNK
    scale = d ** -0.5
    gather = jax.vmap(jax.vmap(lambda blocks, ids: blocks[ids]))

    def to_chunks(t):
        return jnp.moveaxis(t.reshape(bsz, h, n_chunks, MOBA_QCHUNK, t.shape[-1]), 2, 0)

    def chunk(args):
        qc, ic, vc, ci = args
        k_sel = gather(kb, ic)
        v_sel = gather(vb, ic)
        s_sel = jnp.einsum('bhqd,bhqnkd->bhqnk', qc, k_sel).astype(jnp.float32) * scale
        s_sel = jnp.where(vc[..., None], s_sel, -jnp.inf).reshape(bsz, h, MOBA_QCHUNK, n_sel * MOBA_BLOCK)
        own = (ci * MOBA_QCHUNK) // MOBA_BLOCK
        k_own = lax.dynamic_index_in_dim(kb, own, axis=2, keepdims=False)
        v_own = lax.dynamic_index_in_dim(vb, own, axis=2, keepdims=False)
        s_own = jnp.einsum('bhqd,bhkd->bhqk', qc, k_own).astype(jnp.float32) * scale
        q_pos = ci * MOBA_QCHUNK + jnp.arange(MOBA_QCHUNK)
        k_pos = own * MOBA_BLOCK + jnp.arange(MOBA_BLOCK)
        s_own = jnp.where(k_pos[None, :] <= q_pos[:, None], s_own, -jnp.inf)
        p = jax.nn.softmax(jnp.concatenate([s_sel, s_own], axis=-1), axis=-1).astype(v.dtype)
        p_sel = p[..., :n_sel * MOBA_BLOCK].reshape(bsz, h, MOBA_QCHUNK, n_sel, MOBA_BLOCK)
        p_own = p[..., n_sel * MOBA_BLOCK:]
        return (jnp.einsum('bhqnk,bhqnkd->bhqd', p_sel, v_sel)
                + jnp.einsum('bhqk,bhkd->bhqd', p_own, v_own))

    out = lax.map(chunk, (to_chunks(q), to_chunks(top_idx), to_chunks(sel_valid), jnp.arange(n_chunks)))
    return jnp.moveaxis(out, 0, 2).reshape(bsz, h, s, d)


def even_mixer(h, w_in, w_out, a_re, a_im, log_dt, b_re, b_im, c_re, c_im, d_skip, w_glu):
    proj = h @ w_in
    q, k, v, g_sb, u, g_ssm = jnp.split(
        proj, [SB_WIDTH, 2 * SB_WIDTH, 3 * SB_WIDTH, 4 * SB_WIDTH, 4 * SB_WIDTH + SSM_WIDTH], axis=-1)
    o_sb = _merge_heads(stick_breaking_attention(
        _split_heads(q, SB_HEADS), _split_heads(k, SB_HEADS), _split_heads(v, SB_HEADS)))
    o_sb = o_sb * jax.nn.silu(g_sb)
    o_ssm = s5_ssm(u, a_re, a_im, log_dt, b_re, b_im, c_re, c_im, d_skip, w_glu) * jax.nn.silu(g_ssm)
    return jnp.concatenate([o_sb, o_ssm.astype(o_sb.dtype)], axis=-1) @ w_out


def odd_mixer(h, w_in, w_out):
    q, k, v, g = jnp.split(h @ w_in, 4, axis=-1)
    o = _merge_heads(moba_attention(
        _split_heads(q, MOBA_HEADS), _split_heads(k, MOBA_HEADS), _split_heads(v, MOBA_HEADS)))
    return (o * jax.nn.silu(g)) @ w_out


def setup_inputs(seed: int = 0) -> dict:
    key = jax.random.key(seed)
    ks = jax.random.split(key, 20)
    f32 = jnp.float32

    def nrm(k, shape, s):
        return jax.random.normal(k, shape, f32) * s

    ssm_a_im = (math.pi * jnp.arange(SSM_STATE, dtype=f32))[None, None, :] + nrm(ks[7], (N_EVEN, SSM_GROUPS, SSM_STATE), 0.01)
    return {
        'x': nrm(ks[0], (BATCH, SEQ, D_MODEL), 1.0),
        'c': nrm(ks[1], (BATCH, D_MODEL), 1.0),
        'ada_w': nrm(ks[2], (DEPTH, D_MODEL, 3 * D_MODEL), 0.5 * D_MODEL ** -0.5),
        'ada_b': nrm(ks[3], (DEPTH, 3 * D_MODEL), 0.01),
        'ln_g': 1.0 + nrm(ks[4], (DEPTH, D_MODEL), 0.02),
        'ln_b': nrm(ks[5], (DEPTH, D_MODEL), 0.02),
        'even_w_in': nrm(ks[6], (N_EVEN, D_MODEL, EVEN_IN), D_MODEL ** -0.5),
        'even_w_out': nrm(ks[8], (N_EVEN, EVEN_MIX, D_MODEL), EVEN_MIX ** -0.5 * DN_BETA),
        'ssm_a_re': -0.5 + nrm(ks[9], (N_EVEN, SSM_GROUPS, SSM_STATE), 0.01),
        'ssm_a_im': ssm_a_im,
        'ssm_log_dt': jax.random.uniform(ks[10], (N_EVEN, SSM_GROUPS), f32, math.log(1e-3), math.log(1e-1)),
        'ssm_b_re': nrm(ks[11], (N_EVEN, SSM_GROUPS, SSM_STATE, SSM_GROUP), (2 * SSM_GROUP) ** -0.5),
        'ssm_b_im': nrm(ks[12], (N_EVEN, SSM_GROUPS, SSM_STATE, SSM_GROUP), (2 * SSM_GROUP) ** -0.5),
        'ssm_c_re': nrm(ks[13], (N_EVEN, SSM_GROUPS, SSM_GROUP, SSM_STATE), SSM_STATE ** -0.5),
        'ssm_c_im': nrm(ks[14], (N_EVEN, SSM_GROUPS, SSM_GROUP, SSM_STATE), SSM_STATE ** -0.5),
        'ssm_d': nrm(ks[15], (N_EVEN, SSM_WIDTH), 1.0),
        'ssm_w_glu': nrm(ks[16], (N_EVEN, SSM_WIDTH, SSM_WIDTH), SSM_WIDTH ** -0.5),
        'odd_w_in': nrm(ks[17], (N_ODD, D_MODEL, ODD_IN), D_MODEL ** -0.5),
        'odd_w_out': nrm(ks[18], (N_ODD, MOBA_WIDTH, D_MODEL), MOBA_WIDTH ** -0.5 * DN_BETA),
    }


def reference(x, c, ada_w, ada_b, ln_g, ln_b, even_w_in, even_w_out, ssm_a_re, ssm_a_im, ssm_log_dt,
              ssm_b_re, ssm_b_im, ssm_c_re, ssm_c_im, ssm_d, ssm_w_glu, odd_w_in, odd_w_out):
    cond = jax.nn.silu(c)
    for layer in range(DEPTH):
        shift, scale, gate = jnp.split(cond @ ada_w[layer] + ada_b[layer], 3, axis=-1)
        h = x * (1.0 + scale[:, None, :]) + shift[:, None, :]
        i = layer // 2
        if layer % 2 == 0:
            y = even_mixer(h, even_w_in[i], even_w_out[i], ssm_a_re[i], ssm_a_im[i], ssm_log_dt[i],
                           ssm_b_re[i], ssm_b_im[i], ssm_c_re[i], ssm_c_im[i], ssm_d[i], ssm_w_glu[i])
        else:
            y = odd_mixer(h, odd_w_in[i], odd_w_out[i])
        x = _layer_norm(DN_ALPHA * x + (1.0 + gate[:, None, :]) * y.astype(x.dtype), ln_g[layer], ln_b[layer])
    return x
```

```python
import functools
import math

import jax
import jax.numpy as jnp
from jax import lax
from jax.experimental import pallas as pl
from jax.experimental.pallas import tpu as pltpu

F32 = jnp.float32
BF16 = jnp.bfloat16

HEAD_DIM = 128
SB_HEADS = 12
SSM_WIDTH = 512
SSM_GROUP = 16
SSM_GROUPS = 32
SSM_STATE = 64
MOBA_HEADS = 16
MOBA_BLOCK = 256
MOBA_TOPK = 3
LN_EPS = 1e-5
NEG = -1e30

SSM_CHUNK = 64
SSM_FLAT = SSM_CHUNK * SSM_GROUP
SB_TILE = 128
SB_UNDERFLOW = 105.0

VMEM_LIMIT = 56 * 1024 * 1024


def _cparams(n_axes):
    return pltpu.CompilerParams(
        dimension_semantics=("arbitrary",) * n_axes, vmem_limit_bytes=VMEM_LIMIT)


def _ada_kernel(c_ref, w_ref, b_ref, o_ref):
    c = c_ref[...]
    cond = c * jax.nn.sigmoid(c)
    o_ref[0] = jnp.dot(cond, w_ref[0], preferred_element_type=F32,
                       precision=lax.Precision.HIGHEST) + b_ref[0]


def _ada_mod(c_pad, ada_w, ada_b):
    depth, d, n = ada_w.shape
    rows = c_pad.shape[0]
    tn = 1024
    return pl.pallas_call(
        _ada_kernel,
        grid=(depth, n // tn),
        in_specs=[
            pl.BlockSpec((rows, d), lambda l, j: (0, 0)),
            pl.BlockSpec((1, d, tn), lambda l, j: (l, 0, j)),
            pl.BlockSpec((1, 1, tn), lambda l, j: (l, 0, j)),
        ],
        out_specs=pl.BlockSpec((1, rows, tn), lambda l, j: (l, 0, j)),
        out_shape=jax.ShapeDtypeStruct((depth, rows, n), F32),
        compiler_params=_cparams(2),
        name="ada_mod",
    )(c_pad, ada_w, ada_b.reshape(depth, 1, n))


def _in_proj_kernel(x_ref, sc_ref, sh_ref, w_ref, o_ref, h_ref, *, q_tiles, q_scale):
    j = pl.program_id(1)

    @pl.when(j == 0)
    def _():
        h = x_ref[...] * (1.0 + sc_ref[0]) + sh_ref[0]
        h_ref[...] = h.astype(BF16)

    acc = jnp.dot(h_ref[...], w_ref[...], preferred_element_type=F32)
    mult = jnp.where(j < q_tiles, q_scale, 1.0).astype(F32)
    o_ref[...] = (acc * mult).astype(BF16)


def _in_proj(x2, scale, shift, w_bf16, seq, *, q_width, q_scale):
    t, d = x2.shape
    n = w_bf16.shape[1]
    tm, tn = 1024, 512
    per_batch = seq // tm
    kern = functools.partial(_in_proj_kernel, q_tiles=q_width // tn, q_scale=q_scale)
    return pl.pallas_call(
        kern,
        grid=(t // tm, n // tn),
        in_specs=[
            pl.BlockSpec((tm, d), lambda i, j: (i, 0)),
            pl.BlockSpec((1, 1, d), lambda i, j: (i // per_batch, 0, 0)),
            pl.BlockSpec((1, 1, d), lambda i, j: (i // per_batch, 0, 0)),
            pl.BlockSpec((d, tn), lambda i, j: (0, j)),
        ],
        out_specs=pl.BlockSpec((tm, tn), lambda i, j: (i, j)),
        out_shape=jax.ShapeDtypeStruct((t, n), BF16),
        scratch_shapes=[pltpu.VMEM((tm, d), BF16)],
        compiler_params=_cparams(2),
        name="in_proj",
    )(x2, scale, shift, w_bf16)


def _sb_kernel(q_ref, k_ref, v_ref, g_ref, o_ref, *, tile):
    i = pl.program_id(2)
    q = q_ref[...]
    row = lax.broadcasted_iota(jnp.int32, (tile, tile), 0)
    col = lax.broadcasted_iota(jnp.int32, (tile, tile), 1)
    later_key = (row > col).astype(BF16)

    def sweep(blk, carry, acc, past):
        ks = pl.multiple_of(blk * tile, tile)
        kb = k_ref[pl.ds(ks, tile), :]
        vb = v_ref[pl.ds(ks, tile), :]
        z = lax.dot_general(q, kb, (((1,), (1,)), ((), ())), preferred_element_type=F32)
        softplus = jnp.maximum(z, 0.0) + jnp.log(1.0 + jnp.exp(-jnp.abs(z)))
        if past is not None:
            softplus = jnp.where(past, softplus, 0.0)
        within = jnp.dot(softplus.astype(BF16), later_key, preferred_element_type=F32)
        w = jnp.exp(z - softplus - within - carry)
        if past is not None:
            w = jnp.where(past, w, 0.0)
        acc = acc + jnp.dot(w.astype(BF16), vb, preferred_element_type=F32)
        carry = carry + jnp.sum(softplus, axis=1, keepdims=True)
        return carry, acc

    carry0 = jnp.zeros((tile, 1), F32)
    acc0 = jnp.zeros((tile, HEAD_DIM), F32)
    carry, acc = sweep(i, carry0, acc0, col < row)

    def cond(state):
        blk, carry, _ = state
        return jnp.logical_and(blk >= 0, jnp.min(carry) < SB_UNDERFLOW)

    def body(state):
        blk, carry, acc = state
        carry, acc = sweep(blk, carry, acc, None)
        return blk - 1, carry, acc

    _, _, acc = lax.while_loop(cond, body, (i - 1, carry, acc))
    g = g_ref[...].astype(F32)
    o_ref[...] = (acc * (g * jax.nn.sigmoid(g))).astype(BF16)


def _sb_attention(proj, batch, seq):
    tile = SB_TILE
    nq = seq // tile
    h = SB_HEADS
    kern = functools.partial(_sb_kernel, tile=tile)
    return pl.pallas_call(
        kern,
        grid=(batch, h, nq),
        in_specs=[
            pl.BlockSpec((tile, HEAD_DIM), lambda b, hh, i: (b * nq + i, hh)),
            pl.BlockSpec((seq, HEAD_DIM), lambda b, hh, i: (b, h + hh)),
            pl.BlockSpec((seq, HEAD_DIM), lambda b, hh, i: (b, 2 * h + hh)),
            pl.BlockSpec((tile, HEAD_DIM), lambda b, hh, i: (b * nq + i, 3 * h + hh)),
        ],
        out_specs=pl.BlockSpec((tile, HEAD_DIM), lambda b, hh, i: (b * nq + i, hh)),
        out_shape=jax.ShapeDtypeStruct((batch * seq, h * HEAD_DIM), BF16),
        compiler_params=_cparams(3),
        name="sb_attention",
    )(proj, proj, proj, proj)


def _s5_local_kernel(u_ref, mre_ref, mim_ref, sre_ref, sim_ref):
    parts_re, parts_im = [], []
    for g in range(2):
        u = u_ref[g]
        parts_re.append(jnp.dot(u, mre_ref[g], preferred_element_type=F32))
        parts_im.append(jnp.dot(u, mim_ref[g], preferred_element_type=F32))
    sre_ref[...] = jnp.concatenate(parts_re, axis=1)
    sim_ref[...] = jnp.concatenate(parts_im, axis=1)


def _s5_local_state(u_flat, m_in_re, m_in_im):
    groups, rows, flat = u_flat.shape
    p = m_in_re.shape[2]
    return pl.pallas_call(
        _s5_local_kernel,
        grid=(groups // 2,),
        in_specs=[
            pl.BlockSpec((2, rows, flat), lambda g: (g, 0, 0)),
            pl.BlockSpec((2, flat, p), lambda g: (g, 0, 0)),
            pl.BlockSpec((2, flat, p), lambda g: (g, 0, 0)),
        ],
        out_specs=[
            pl.BlockSpec((rows, 2 * p), lambda g: (0, g)),
            pl.BlockSpec((rows, 2 * p), lambda g: (0, g)),
        ],
        out_shape=[jax.ShapeDtypeStruct((rows, groups * p), F32)] * 2,
        compiler_params=_cparams(1),
        name="s5_local_state",
    )(u_flat, m_in_re, m_in_im)


def _s5_scan_kernel(sre_ref, sim_ref, are_ref, aim_ref, ore_ref, oim_ref):
    n_chunks, batch, width = sre_ref.shape
    a_re = are_ref[...]
    a_im = aim_ref[...]

    def body(c, state):
        st_re, st_im = state
        ore_ref[c] = st_re
        oim_ref[c] = st_im
        new_re = a_re * st_re - a_im * st_im + sre_ref[c]
        new_im = a_re * st_im + a_im * st_re + sim_ref[c]
        return new_re, new_im

    zero = jnp.zeros((batch, width), F32)
    lax.fori_loop(0, n_chunks, body, (zero, zero))


def _s5_scan(s_re, s_im, a_re, a_im, batch):
    rows, width = s_re.shape
    n_chunks = rows // batch
    wt = 512
    blk = pl.BlockSpec((n_chunks, batch, wt), lambda j: (0, 0, j))
    vec = pl.BlockSpec((1, wt), lambda j: (0, j))
    prev_re, prev_im = pl.pallas_call(
        _s5_scan_kernel,
        grid=(width // wt,),
        in_specs=[blk, blk, vec, vec],
        out_specs=[blk, blk],
        out_shape=[jax.ShapeDtypeStruct((n_chunks, batch, width), F32)] * 2,
        compiler_params=_cparams(1),
        name="s5_scan",
    )(s_re.reshape(n_chunks, batch, width), s_im.reshape(n_chunks, batch, width), a_re, a_im)
    return prev_re.reshape(rows, width), prev_im.reshape(rows, width)


def _s5_output_kernel(u_ref, t_ref, pre_ref, pim_ref, ore_ref, oim_ref, y_ref):
    flat = t_ref.shape[1]
    for g in range(2):
        y_ref[0, :, g * flat:(g + 1) * flat] = jnp.dot(u_ref[g], t_ref[g], preferred_element_type=F32)
    y_ref[0] += (jnp.dot(pre_ref[...].astype(BF16), ore_ref[0], preferred_element_type=F32)
                 + jnp.dot(pim_ref[...].astype(BF16), oim_ref[0], preferred_element_type=F32))


def _s5_output(u_flat, toeplitz, prev_re, prev_im, m_out_re, m_out_im):
    groups, rows, flat = u_flat.shape
    pp = m_out_re.shape[1]
    return pl.pallas_call(
        _s5_output_kernel,
        grid=(groups // 2,),
        in_specs=[
            pl.BlockSpec((2, rows, flat), lambda g: (g, 0, 0)),
            pl.BlockSpec((2, flat, flat), lambda g: (g, 0, 0)),
            pl.BlockSpec((rows, pp), lambda g: (0, g)),
            pl.BlockSpec((rows, pp), lambda g: (0, g)),
            pl.BlockSpec((1, pp, 2 * flat), lambda g: (g, 0, 0)),
            pl.BlockSpec((1, pp, 2 * flat), lambda g: (g, 0, 0)),
        ],
        out_specs=pl.BlockSpec((1, rows, 2 * flat), lambda g: (g, 0, 0)),
        out_shape=jax.ShapeDtypeStruct((groups // 2, rows, 2 * flat), F32),
        compiler_params=_cparams(1),
        name="s5_output",
    )(u_flat, toeplitz, prev_re, prev_im, m_out_re, m_out_im)


def _s5_epilogue_kernel(y_ref, u_ref, g_ref, d_ref, w_ref, o_ref):
    y = y_ref[...] + d_ref[...] * u_ref[...].astype(F32)
    y = jax.nn.gelu(y)
    gate = jnp.dot(y.astype(BF16), w_ref[...], preferred_element_type=F32)
    y = y * jax.nn.sigmoid(gate)
    g = g_ref[...].astype(F32)
    o_ref[...] = (y * (g * jax.nn.sigmoid(g))).astype(BF16)


def _s5_epilogue(y, proj, d_skip, w_glu_bf16, u_col, g_col):
    t, w = y.shape
    tm = 1024
    return pl.pallas_call(
        _s5_epilogue_kernel,
        grid=(t // tm,),
        in_specs=[
            pl.BlockSpec((tm, w), lambda i: (i, 0)),
            pl.BlockSpec((tm, w), lambda i: (i, u_col)),
            pl.BlockSpec((tm, w), lambda i: (i, g_col)),
            pl.BlockSpec((1, w), lambda i: (0, 0)),
            pl.BlockSpec((w, w), lambda i: (0, 0)),
        ],
        out_specs=pl.BlockSpec((tm, w), lambda i: (i, 0)),
        out_shape=jax.ShapeDtypeStruct((t, w), BF16),
        compiler_params=_cparams(1),
        name="s5_epilogue",
    )(y, proj, proj, d_skip.reshape(1, w), w_glu_bf16)


def _s5_operators(a_re, a_im, log_dt, b_re, b_im, c_re, c_im):
    hi = lax.Precision.HIGHEST
    length = SSM_CHUNK
    lam = lax.complex(a_re.astype(F32), a_im.astype(F32))
    dt = jnp.exp(log_dt.astype(F32))[:, None]
    a_bar = jnp.exp(lam * dt)
    b_bar = ((a_bar - 1.0) / lam)[..., None] * lax.complex(b_re.astype(F32), b_im.astype(F32))
    c_c = lax.complex(c_re.astype(F32), c_im.astype(F32))
    steps = jnp.arange(length + 1, dtype=F32)
    powers = jnp.exp((lam * dt)[:, None, :] * steps[None, :, None])
    c_pow = c_c[:, None, :, :] * powers[:, :, None, :]

    def re_prod(x, y, spec):
        return (jnp.einsum(spec, jnp.real(x), jnp.real(y), precision=hi)
                - jnp.einsum(spec, jnp.imag(x), jnp.imag(y), precision=hi))

    lag_kernel = re_prod(c_pow[:, :length], b_bar, 'gnip,gpj->gnij')
    s_idx = jnp.arange(length)[:, None]
    t_idx = jnp.arange(length)[None, :]
    lag = t_idx - s_idx
    toe = jnp.where((lag >= 0)[None, :, :, None, None],
                    lag_kernel[:, jnp.clip(lag, 0, length - 1)], 0.0)
    toeplitz = toe.transpose(0, 1, 4, 2, 3).reshape(SSM_GROUPS, SSM_FLAT, SSM_FLAT)
    m_in = powers[:, length - 1 - jnp.arange(length), :, None] * b_bar[:, None, :, :]
    m_in = m_in.transpose(0, 1, 3, 2).reshape(SSM_GROUPS, SSM_FLAT, SSM_STATE)
    m_out = c_pow[:, 1:].transpose(0, 3, 1, 2).reshape(SSM_GROUPS, SSM_STATE, SSM_FLAT)

    def pair_blockdiag(m):
        m = m.reshape(SSM_GROUPS // 2, 2, SSM_STATE, SSM_FLAT)
        z = jnp.zeros_like(m[:, 0])
        top = jnp.concatenate([m[:, 0], z], axis=2)
        bot = jnp.concatenate([z, m[:, 1]], axis=2)
        return jnp.concatenate([top, bot], axis=1)

    a_chunk = powers[:, length]
    return dict(
        toeplitz=toeplitz.astype(BF16),
        m_in_re=jnp.real(m_in).astype(BF16), m_in_im=jnp.imag(m_in).astype(BF16),
        m_out_re=pair_blockdiag(jnp.real(m_out)).astype(BF16),
        m_out_im=pair_blockdiag(-jnp.imag(m_out)).astype(BF16),
        a_re=jnp.real(a_chunk).reshape(1, -1), a_im=jnp.imag(a_chunk).reshape(1, -1),
    )


def _s5_ssm(proj, batch, seq, ops, d_skip, w_glu_bf16, u_col0):
    t = batch * seq
    n_chunks = seq // SSM_CHUNK
    n_rows = batch * n_chunks
    u = proj[:, u_col0:u_col0 + SSM_WIDTH]
    u_flat = (u.reshape(batch, n_chunks, SSM_CHUNK, SSM_GROUPS, SSM_GROUP)
              .transpose(3, 1, 0, 2, 4).reshape(SSM_GROUPS, n_rows, SSM_FLAT))
    s_re, s_im = _s5_local_state(u_flat, ops['m_in_re'], ops['m_in_im'])
    prev_re, prev_im = _s5_scan(s_re, s_im, ops['a_re'], ops['a_im'], batch)
    y_flat = _s5_output(u_flat, ops['toeplitz'], prev_re, prev_im, ops['m_out_re'], ops['m_out_im'])
    y = (y_flat.reshape(SSM_GROUPS // 2, n_chunks, batch, 2, SSM_CHUNK, SSM_GROUP)
         .transpose(2, 1, 4, 0, 3, 5).reshape(t, SSM_WIDTH))
    return _s5_epilogue(y, proj, d_skip, w_glu_bf16, u_col0 // SSM_WIDTH, u_col0 // SSM_WIDTH + 1)


def _moba_kernel(q_ref, k_ref, v_ref, g_ref, o_ref, kaug_ref, kmean_ref, *, seq, q_scale):
    blk = MOBA_BLOCK
    nb = seq // blk
    i = pl.program_id(2)

    @pl.when(i == 0)
    def _():
        kmean_ref[...] = jnp.zeros_like(kmean_ref)
        block_lane = lax.broadcasted_iota(jnp.int32, (blk, HEAD_DIM), 1)

        def mean_body(j, _):
            ks = pl.multiple_of(j * blk, blk)
            rows = k_ref[pl.ds(ks, blk), :]
            kaug_ref[pl.ds(ks, blk), :HEAD_DIM] = rows
            kaug_ref[pl.ds(ks, blk), HEAD_DIM:] = (block_lane == j).astype(BF16)
            kmean_ref[pl.ds(j, 1), :] = jnp.sum(rows.astype(F32), axis=0, keepdims=True) * (1.0 / blk)
            return 0

        lax.fori_loop(0, nb, mean_body, 0)

    q = q_ref[...]
    kmean = kmean_ref[...]
    km_hi = kmean.astype(BF16)
    km_lo = (kmean - km_hi.astype(F32)).astype(BF16)
    contract_last = (((1,), (1,)), ((), ()))
    gate = (lax.dot_general(q, km_hi, contract_last, preferred_element_type=F32)
            + lax.dot_general(q, km_lo, contract_last, preferred_element_type=F32))
    lane = lax.broadcasted_iota(jnp.int32, (blk, HEAD_DIM), 1)
    lane_f = lane.astype(F32)
    valid = jnp.logical_and(lane < i, gate > 0.5 * NEG * q_scale)
    live = jnp.where(valid, gate, -jnp.inf)
    selected = jnp.zeros((blk, HEAD_DIM), jnp.bool_)
    for _ in range(MOBA_TOPK):
        top = jnp.max(live, axis=1, keepdims=True)
        hit = jnp.logical_and(live == top, top > -jnp.inf)
        first = jnp.min(jnp.where(hit, lane_f, 1e9), axis=1, keepdims=True)
        pick = lane_f == first
        selected = jnp.logical_or(selected, pick)
        live = jnp.where(pick, -jnp.inf, live)
    bias = jnp.where(selected, 0.0, NEG).astype(BF16)
    q_aug = jnp.concatenate([q, bias], axis=1)

    own = pl.multiple_of(i * blk, blk)
    s = lax.dot_general(q, k_ref[pl.ds(own, blk), :], contract_last, preferred_element_type=F32)
    row = lax.broadcasted_iota(jnp.int32, (blk, blk), 0)
    col = lax.broadcasted_iota(jnp.int32, (blk, blk), 1)
    s = jnp.where(col <= row, s, -jnp.inf)
    m = jnp.max(s, axis=1, keepdims=True)
    p = jnp.exp(s - m)
    l = jnp.sum(p, axis=1, keepdims=True)
    acc = jnp.dot(p.astype(BF16), v_ref[pl.ds(own, blk), :], preferred_element_type=F32)

    def body(j, state):
        m, l, acc = state
        ks = pl.multiple_of(j * blk, blk)
        s = lax.dot_general(q_aug, kaug_ref[pl.ds(ks, blk), :], contract_last,
                            preferred_element_type=F32)
        m_new = jnp.maximum(m, jnp.max(s, axis=1, keepdims=True))
        alpha = jnp.exp(m - m_new)
        p = jnp.exp(s - m_new)
        l = alpha * l + jnp.sum(p, axis=1, keepdims=True)
        acc = alpha * acc + jnp.dot(p.astype(BF16), v_ref[pl.ds(ks, blk), :],
                                    preferred_element_type=F32)
        return m_new, l, acc

    m, l, acc = lax.fori_loop(0, i, body, (m, l, acc))
    g = g_ref[...].astype(F32)
    o_ref[...] = (acc / l * (g * jax.nn.sigmoid(g))).astype(BF16)


def _moba_attention(proj, batch, seq, q_scale):
    blk = MOBA_BLOCK
    nq = seq // blk
    h = MOBA_HEADS
    kern = functools.partial(_moba_kernel, seq=seq, q_scale=q_scale)
    return pl.pallas_call(
        kern,
        grid=(batch, h, nq),
        in_specs=[
            pl.BlockSpec((blk, HEAD_DIM), lambda b, hh, i: (b * nq + i, hh)),
            pl.BlockSpec((seq, HEAD_DIM), lambda b, hh, i: (b, h + hh)),
            pl.BlockSpec((seq, HEAD_DIM), lambda b, hh, i: (b, 2 * h + hh)),
            pl.BlockSpec((blk, HEAD_DIM), lambda b, hh, i: (b * nq + i, 3 * h + hh)),
        ],
        out_specs=pl.BlockSpec((blk, HEAD_DIM), lambda b, hh, i: (b * nq + i, hh)),
        out_shape=jax.ShapeDtypeStruct((batch * seq, h * HEAD_DIM), BF16),
        scratch_shapes=[pltpu.VMEM((seq, 2 * HEAD_DIM), BF16), pltpu.VMEM((HEAD_DIM, HEAD_DIM), F32)],
        compiler_params=_cparams(3),
        name="moba_attention",
    )(proj, proj, proj, proj)


def _out_proj_kernel(*refs, n_in, alpha):
    a_refs = refs[:n_in]
    w_refs = refs[n_in:2 * n_in]
    x_ref, gate_ref, lng_ref, lnb_ref, o_ref = refs[2 * n_in:]
    y = jnp.dot(a_refs[0][...], w_refs[0][...], preferred_element_type=F32)
    for a_ref, w_ref in zip(a_refs[1:], w_refs[1:]):
        y = y + jnp.dot(a_ref[...], w_ref[...], preferred_element_type=F32)
    z = alpha * x_ref[...] + (1.0 + gate_ref[0]) * y
    mu = jnp.mean(z, axis=-1, keepdims=True)
    zc = z - mu
    var = jnp.mean(zc * zc, axis=-1, keepdims=True)
    o_ref[...] = zc * lax.rsqrt(var + LN_EPS) * lng_ref[...] + lnb_ref[...]


def _out_proj_ln(acts, weights, x2, gate, ln_g, ln_b, seq, alpha):
    t, d = x2.shape
    tm = 512
    per_batch = seq // tm
    n_in = len(acts)
    kern = functools.partial(_out_proj_kernel, n_in=n_in, alpha=alpha)
    in_specs = ([pl.BlockSpec((tm, a.shape[1]), lambda i: (i, 0)) for a in acts]
                + [pl.BlockSpec(w.shape, lambda i: (0, 0)) for w in weights]
                + [pl.BlockSpec((tm, d), lambda i: (i, 0)),
                   pl.BlockSpec((1, 1, d), lambda i: (i // per_batch, 0, 0)),
                   pl.BlockSpec((1, d), lambda i: (0, 0)),
                   pl.BlockSpec((1, d), lambda i: (0, 0))])
    return pl.pallas_call(
        kern,
        grid=(t // tm,),
        in_specs=in_specs,
        out_specs=pl.BlockSpec((tm, d), lambda i: (i, 0)),
        out_shape=jax.ShapeDtypeStruct((t, d), F32),
        compiler_params=_cparams(1),
        name="out_proj_ln",
    )(*acts, *weights, x2, gate, ln_g.reshape(1, d), ln_b.reshape(1, d))


def kernel(x, c, ada_w, ada_b, ln_g, ln_b, even_w_in, even_w_out, ssm_a_re, ssm_a_im, ssm_log_dt,
           ssm_b_re, ssm_b_im, ssm_c_re, ssm_c_im, ssm_d, ssm_w_glu, odd_w_in, odd_w_out):
    batch, seq, d = x.shape
    depth = ada_w.shape[0]
    alpha = (2 * depth) ** 0.25
    q_scale = HEAD_DIM ** -0.5
    sb_width = SB_HEADS * HEAD_DIM

    c_pad = jnp.pad(c, ((0, 8 - batch), (0, 0)))
    mods = _ada_mod(c_pad, ada_w, ada_b)[:, :batch]
    x2 = x.reshape(batch * seq, d)
    for layer in range(depth):
        shift = mods[layer, :, None, :d]
        scale = mods[layer, :, None, d:2 * d]
        gate = mods[layer, :, None, 2 * d:]
        idx = layer // 2
        if layer % 2 == 0:
            proj = _in_proj(x2, scale, shift, even_w_in[idx].astype(BF16), seq,
                            q_width=sb_width, q_scale=q_scale)
            o_sb = _sb_attention(proj, batch, seq)
            ops = _s5_operators(ssm_a_re[idx], ssm_a_im[idx], ssm_log_dt[idx], ssm_b_re[idx],
                                ssm_b_im[idx], ssm_c_re[idx], ssm_c_im[idx])
            o_ssm = _s5_ssm(proj, batch, seq, ops, ssm_d[idx], ssm_w_glu[idx].astype(BF16),
                            4 * sb_width)
            w_out = even_w_out[idx].astype(BF16)
            acts, weights = [o_sb, o_ssm], [w_out[:sb_width], w_out[sb_width:]]
        else:
            proj = _in_proj(x2, scale, shift, odd_w_in[idx].astype(BF16), seq,
                            q_width=MOBA_HEADS * HEAD_DIM, q_scale=q_scale)
            o = _moba_attention(proj, batch, seq, q_scale)
            acts, weights = [o], [odd_w_out[idx].astype(BF16)]
        x2 = _out_proj_ln(acts, weights, x2, gate, ln_g[layer], ln_b[layer], seq, alpha)
    return x2.reshape(batch, seq, d)
```

```python
import functools
import math

import jax
import jax.numpy as jnp
from jax import lax
from jax.experimental import pallas as pl
from jax.experimental.pallas import tpu as pltpu

F32 = jnp.float32
BF16 = jnp.bfloat16

HEAD_DIM = 128
SB_HEADS = 12
SSM_WIDTH = 512
SSM_GROUP = 16
SSM_GROUPS = 32
SSM_STATE = 64
MOBA_HEADS = 16
MOBA_BLOCK = 256
MOBA_TOPK = 3
LN_EPS = 1e-5
NEG = -1e30

SSM_CHUNK = 64
SSM_FLAT = SSM_CHUNK * SSM_GROUP
SB_TILE = 256
MOBA_QBLOCKS = 2
MOBA_KCHUNK = 4
MOBA_ROWS = 128
SB_UNDERFLOW = 105.0

VMEM_LIMIT = 56 * 1024 * 1024


def _cparams(n_axes):
    return pltpu.CompilerParams(
        dimension_semantics=("arbitrary",) * n_axes, vmem_limit_bytes=VMEM_LIMIT)


def _ada_kernel(c_ref, w_ref, b_ref, o_ref):
    c = c_ref[...]
    cond = c * jax.nn.sigmoid(c)
    o_ref[0] = jnp.dot(cond, w_ref[0], preferred_element_type=F32,
                       precision=lax.Precision.HIGHEST) + b_ref[0]


def _ada_mod(c_pad, ada_w, ada_b):
    depth, d, n = ada_w.shape
    rows = c_pad.shape[0]
    tn = 1024
    return pl.pallas_call(
        _ada_kernel,
        grid=(depth, n // tn),
        in_specs=[
            pl.BlockSpec((rows, d), lambda l, j: (0, 0)),
            pl.BlockSpec((1, d, tn), lambda l, j: (l, 0, j)),
            pl.BlockSpec((1, 1, tn), lambda l, j: (l, 0, j)),
        ],
        out_specs=pl.BlockSpec((1, rows, tn), lambda l, j: (l, 0, j)),
        out_shape=jax.ShapeDtypeStruct((depth, rows, n), F32),
        compiler_params=_cparams(2),
        name="ada_mod",
    )(c_pad, ada_w, ada_b.reshape(depth, 1, n))


def _in_proj_kernel(x_ref, sc_ref, sh_ref, w_ref, o_ref, h_ref, *, q_tiles, q_scale):
    j = pl.program_id(1)

    @pl.when(j == 0)
    def _():
        h = x_ref[...] * (1.0 + sc_ref[0]) + sh_ref[0]
        h_ref[...] = h.astype(BF16)

    acc = jnp.dot(h_ref[...], w_ref[...], preferred_element_type=F32)
    mult = jnp.where(j < q_tiles, q_scale, 1.0).astype(F32)
    o_ref[...] = (acc * mult).astype(BF16)


def _in_proj(x2, scale, shift, w_bf16, seq, *, q_width, q_scale):
    t, d = x2.shape
    n = w_bf16.shape[1]
    tm, tn = 1024, 512
    per_batch = seq // tm
    kern = functools.partial(_in_proj_kernel, q_tiles=q_width // tn, q_scale=q_scale)
    return pl.pallas_call(
        kern,
        grid=(t // tm, n // tn),
        in_specs=[
            pl.BlockSpec((tm, d), lambda i, j: (i, 0)),
            pl.BlockSpec((1, 1, d), lambda i, j: (i // per_batch, 0, 0)),
            pl.BlockSpec((1, 1, d), lambda i, j: (i // per_batch, 0, 0)),
            pl.BlockSpec((d, tn), lambda i, j: (0, j)),
        ],
        out_specs=pl.BlockSpec((tm, tn), lambda i, j: (i, j)),
        out_shape=jax.ShapeDtypeStruct((t, n), BF16),
        scratch_shapes=[pltpu.VMEM((tm, d), BF16)],
        compiler_params=_cparams(2),
        name="in_proj",
    )(x2, scale, shift, w_bf16)


def _sb_kernel(q_ref, k_ref, v_ref, g_ref, o_ref, *, tile):
    i = pl.program_id(2)
    q = q_ref[...]
    row = lax.broadcasted_iota(jnp.int32, (tile, tile), 0)
    col = lax.broadcasted_iota(jnp.int32, (tile, tile), 1)
    later_key = (row > col).astype(BF16)

    def sweep(blocks, pasts, carry):
        starts = [pl.multiple_of(blk * tile, tile) for blk in blocks]
        zs = [lax.dot_general(q, k_ref[pl.ds(ks, tile), :], (((1,), (1,)), ((), ())),
                              preferred_element_type=F32) for ks in starts]
        softplus = []
        for z, past in zip(zs, pasts):
            sp = jnp.maximum(z, 0.0) + jnp.log(1.0 + jnp.exp(-jnp.abs(z)))
            softplus.append(sp if past is None else jnp.where(past, sp, 0.0))
        within = [jnp.dot(sp.astype(BF16), later_key, preferred_element_type=F32) for sp in softplus]
        weights = []
        for z, sp, inner, past in zip(zs, softplus, within, pasts):
            w = jnp.exp(z - sp - inner - carry)
            weights.append((w if past is None else jnp.where(past, w, 0.0)).astype(BF16))
            carry = carry + jnp.sum(sp, axis=1, keepdims=True)
        out = None
        for w, ks in zip(weights, starts):
            part = jnp.dot(w, v_ref[pl.ds(ks, tile), :], preferred_element_type=F32)
            out = part if out is None else out + part
        return carry, out

    has_prev = jnp.broadcast_to(i > 0, (tile, tile))
    carry, acc = sweep([i, jnp.maximum(i - 1, 0)], [col < row, has_prev], jnp.zeros((tile, 1), F32))

    def cond(state):
        blk, carry, _ = state
        return jnp.logical_and(blk >= 0, jnp.min(carry) < SB_UNDERFLOW)

    def body(state):
        blk, carry, acc = state
        carry, out = sweep([blk], [None], carry)
        return blk - 1, carry, acc + out

    _, _, acc = lax.while_loop(cond, body, (i - 2, carry, acc))
    g = g_ref[...].astype(F32)
    o_ref[...] = (acc * (g * jax.nn.sigmoid(g))).astype(BF16)


def _sb_attention(proj, batch, seq):
    tile = SB_TILE
    nq = seq // tile
    h = SB_HEADS
    kern = functools.partial(_sb_kernel, tile=tile)
    return pl.pallas_call(
        kern,
        grid=(batch, h, nq),
        in_specs=[
            pl.BlockSpec((tile, HEAD_DIM), lambda b, hh, i: (b * nq + i, hh)),
            pl.BlockSpec((seq, HEAD_DIM), lambda b, hh, i: (b, h + hh)),
            pl.BlockSpec((seq, HEAD_DIM), lambda b, hh, i: (b, 2 * h + hh)),
            pl.BlockSpec((tile, HEAD_DIM), lambda b, hh, i: (b * nq + i, 3 * h + hh)),
        ],
        out_specs=pl.BlockSpec((tile, HEAD_DIM), lambda b, hh, i: (b * nq + i, hh)),
        out_shape=jax.ShapeDtypeStruct((batch * seq, h * HEAD_DIM), BF16),
        compiler_params=_cparams(3),
        name="sb_attention",
    )(proj, proj, proj, proj)


def _s5_local_kernel(u_ref, mre_ref, mim_ref, sre_ref, sim_ref):
    parts_re, parts_im = [], []
    for g in range(2):
        u = u_ref[g]
        parts_re.append(jnp.dot(u, mre_ref[g], preferred_element_type=F32))
        parts_im.append(jnp.dot(u, mim_ref[g], preferred_element_type=F32))
    sre_ref[...] = jnp.concatenate(parts_re, axis=1)
    sim_ref[...] = jnp.concatenate(parts_im, axis=1)


def _s5_local_state(u_flat, m_in_re, m_in_im):
    groups, rows, flat = u_flat.shape
    p = m_in_re.shape[2]
    return pl.pallas_call(
        _s5_local_kernel,
        grid=(groups // 2,),
        in_specs=[
            pl.BlockSpec((2, rows, flat), lambda g: (g, 0, 0)),
            pl.BlockSpec((2, flat, p), lambda g: (g, 0, 0)),
            pl.BlockSpec((2, flat, p), lambda g: (g, 0, 0)),
        ],
        out_specs=[
            pl.BlockSpec((rows, 2 * p), lambda g: (0, g)),
            pl.BlockSpec((rows, 2 * p), lambda g: (0, g)),
        ],
        out_shape=[jax.ShapeDtypeStruct((rows, groups * p), F32)] * 2,
        compiler_params=_cparams(1),
        name="s5_local_state",
    )(u_flat, m_in_re, m_in_im)


def _s5_scan_kernel(sre_ref, sim_ref, are_ref, aim_ref, ore_ref, oim_ref):
    n_chunks, batch, width = sre_ref.shape
    a_re = are_ref[...]
    a_im = aim_ref[...]

    def body(c, state):
        st_re, st_im = state
        ore_ref[c] = st_re
        oim_ref[c] = st_im
        new_re = a_re * st_re - a_im * st_im + sre_ref[c]
        new_im = a_re * st_im + a_im * st_re + sim_ref[c]
        return new_re, new_im

    zero = jnp.zeros((batch, width), F32)
    lax.fori_loop(0, n_chunks, body, (zero, zero))


def _s5_scan(s_re, s_im, a_re, a_im, batch):
    rows, width = s_re.shape
    n_chunks = rows // batch
    wt = 512
    blk = pl.BlockSpec((n_chunks, batch, wt), lambda j: (0, 0, j))
    vec = pl.BlockSpec((1, wt), lambda j: (0, j))
    prev_re, prev_im = pl.pallas_call(
        _s5_scan_kernel,
        grid=(width // wt,),
        in_specs=[blk, blk, vec, vec],
        out_specs=[blk, blk],
        out_shape=[jax.ShapeDtypeStruct((n_chunks, batch, width), F32)] * 2,
        compiler_params=_cparams(1),
        name="s5_scan",
    )(s_re.reshape(n_chunks, batch, width), s_im.reshape(n_chunks, batch, width), a_re, a_im)
    return prev_re.reshape(rows, width), prev_im.reshape(rows, width)


def _s5_output_kernel(u_ref, t_ref, pre_ref, pim_ref, ore_ref, oim_ref, y_ref):
    flat = t_ref.shape[1]
    carried = (jnp.dot(pre_ref[...].astype(BF16), ore_ref[0], preferred_element_type=F32)
               + jnp.dot(pim_ref[...].astype(BF16), oim_ref[0], preferred_element_type=F32))
    for g in range(2):
        cols = slice(g * flat, (g + 1) * flat)
        local = jnp.dot(u_ref[g], t_ref[g], preferred_element_type=F32)
        y_ref[0, :, cols] = (local + carried[:, cols]).astype(BF16)


def _s5_output(u_flat, toeplitz, prev_re, prev_im, m_out_re, m_out_im):
    groups, rows, flat = u_flat.shape
    pp = m_out_re.shape[1]
    return pl.pallas_call(
        _s5_output_kernel,
        grid=(groups // 2,),
        in_specs=[
            pl.BlockSpec((2, rows, flat), lambda g: (g, 0, 0)),
            pl.BlockSpec((2, flat, flat), lambda g: (g, 0, 0)),
            pl.BlockSpec((rows, pp), lambda g: (0, g)),
            pl.BlockSpec((rows, pp), lambda g: (0, g)),
            pl.BlockSpec((1, pp, 2 * flat), lambda g: (g, 0, 0)),
            pl.BlockSpec((1, pp, 2 * flat), lambda g: (g, 0, 0)),
        ],
        out_specs=pl.BlockSpec((1, rows, 2 * flat), lambda g: (g, 0, 0)),
        out_shape=jax.ShapeDtypeStruct((groups // 2, rows, 2 * flat), BF16),
        compiler_params=_cparams(1),
        name="s5_output",
    )(u_flat, toeplitz, prev_re, prev_im, m_out_re, m_out_im)


def _s5_epilogue_kernel(y_ref, u_ref, g_ref, d_ref, w_ref, o_ref):
    y = y_ref[...].astype(F32) + d_ref[...] * u_ref[...].astype(F32)
    y = jax.nn.gelu(y)
    gate = jnp.dot(y.astype(BF16), w_ref[...], preferred_element_type=F32)
    y = y * jax.nn.sigmoid(gate)
    g = g_ref[...].astype(F32)
    o_ref[...] = (y * (g * jax.nn.sigmoid(g))).astype(BF16)


def _s5_epilogue(y, proj, d_skip, w_glu_bf16, u_col, g_col):
    t, w = y.shape
    tm = 1024
    return pl.pallas_call(
        _s5_epilogue_kernel,
        grid=(t // tm,),
        in_specs=[
            pl.BlockSpec((tm, w), lambda i: (i, 0)),
            pl.BlockSpec((tm, w), lambda i: (i, u_col)),
            pl.BlockSpec((tm, w), lambda i: (i, g_col)),
            pl.BlockSpec((1, w), lambda i: (0, 0)),
            pl.BlockSpec((w, w), lambda i: (0, 0)),
        ],
        out_specs=pl.BlockSpec((tm, w), lambda i: (i, 0)),
        out_shape=jax.ShapeDtypeStruct((t, w), BF16),
        compiler_params=_cparams(1),
        name="s5_epilogue",
    )(y, proj, proj, d_skip.reshape(1, w), w_glu_bf16)


def _s5_operators(a_re, a_im, log_dt, b_re, b_im, c_re, c_im):
    hi = lax.Precision.HIGHEST
    length = SSM_CHUNK
    lam_re, lam_im = a_re.astype(F32), a_im.astype(F32)
    dt = jnp.exp(log_dt.astype(F32))[:, None]
    steps = jnp.arange(length + 1, dtype=F32)[None, :, None]
    mag = jnp.exp((lam_re * dt)[:, None, :] * steps)
    ang = (lam_im * dt)[:, None, :] * steps
    pw_re, pw_im = mag * jnp.cos(ang), mag * jnp.sin(ang)
    num_re, num_im = pw_re[:, 1] - 1.0, pw_im[:, 1]
    den = lam_re * lam_re + lam_im * lam_im
    quo_re = (num_re * lam_re + num_im * lam_im) / den
    quo_im = (num_im * lam_re - num_re * lam_im) / den
    b_re, b_im = b_re.astype(F32), b_im.astype(F32)
    bb_re = quo_re[..., None] * b_re - quo_im[..., None] * b_im
    bb_im = quo_re[..., None] * b_im + quo_im[..., None] * b_re
    c_re, c_im = c_re.astype(F32)[:, None], c_im.astype(F32)[:, None]
    cp_re = c_re * pw_re[:, :, None, :] - c_im * pw_im[:, :, None, :]
    cp_im = c_re * pw_im[:, :, None, :] + c_im * pw_re[:, :, None, :]

    lag_kernel = (jnp.einsum('gnip,gpj->gjni', cp_re[:, :length], bb_re, precision=hi)
                  - jnp.einsum('gnip,gpj->gjni', cp_im[:, :length], bb_im, precision=hi)).astype(BF16)
    period = jnp.concatenate([lag_kernel, jnp.zeros_like(lag_kernel), jnp.zeros_like(lag_kernel[:, :, :1])],
                             axis=2)
    skew = jnp.tile(period, (1, 1, length, 1))[:, :, :2 * length * length]
    skew = skew.reshape(SSM_GROUPS, SSM_GROUP, length, 2 * length, SSM_GROUP)[:, :, :, :length]
    toeplitz = skew.transpose(0, 2, 1, 3, 4).reshape(SSM_GROUPS, SSM_FLAT, SSM_FLAT)

    rev = length - 1 - jnp.arange(length)
    in_re = pw_re[:, rev, :, None] * bb_re[:, None] - pw_im[:, rev, :, None] * bb_im[:, None]
    in_im = pw_re[:, rev, :, None] * bb_im[:, None] + pw_im[:, rev, :, None] * bb_re[:, None]

    def flat_in(m):
        return m.transpose(0, 1, 3, 2).reshape(SSM_GROUPS, SSM_FLAT, SSM_STATE).astype(BF16)

    def flat_out(m):
        m = m.transpose(0, 3, 1, 2).reshape(SSM_GROUPS // 2, 2, SSM_STATE, SSM_FLAT)
        z = jnp.zeros_like(m[:, 0])
        top = jnp.concatenate([m[:, 0], z], axis=2)
        bot = jnp.concatenate([z, m[:, 1]], axis=2)
        return jnp.concatenate([top, bot], axis=1).astype(BF16)

    return dict(
        toeplitz=toeplitz,
        m_in_re=flat_in(in_re), m_in_im=flat_in(in_im),
        m_out_re=flat_out(cp_re[:, 1:]), m_out_im=flat_out(-cp_im[:, 1:]),
        a_re=pw_re[:, length].reshape(1, -1), a_im=pw_im[:, length].reshape(1, -1),
    )


def _s5_ssm(proj, batch, seq, ops, d_skip, w_glu_bf16, u_col0):
    t = batch * seq
    n_chunks = seq // SSM_CHUNK
    n_rows = batch * n_chunks
    u = proj[:, u_col0:u_col0 + SSM_WIDTH]
    u_flat = (u.reshape(batch, n_chunks, SSM_CHUNK, SSM_GROUPS, SSM_GROUP)
              .transpose(3, 1, 0, 2, 4).reshape(SSM_GROUPS, n_rows, SSM_FLAT))
    s_re, s_im = _s5_local_state(u_flat, ops['m_in_re'], ops['m_in_im'])
    prev_re, prev_im = _s5_scan(s_re, s_im, ops['a_re'], ops['a_im'], batch)
    y_flat = _s5_output(u_flat, ops['toeplitz'], prev_re, prev_im, ops['m_out_re'], ops['m_out_im'])
    y = (y_flat.reshape(SSM_GROUPS // 2, n_chunks, batch, 2, SSM_CHUNK, SSM_GROUP)
         .transpose(2, 1, 4, 0, 3, 5).reshape(t, SSM_WIDTH))
    return _s5_epilogue(y, proj, d_skip, w_glu_bf16, u_col0 // SSM_WIDTH, u_col0 // SSM_WIDTH + 1)


def _moba_kernel(q_ref, k_ref, v_ref, g_ref, o_ref, kaug_ref, kmean_ref, *, seq, q_scale):
    blk = MOBA_BLOCK
    nb = seq // blk
    i = pl.program_id(2)

    @pl.when(i == 0)
    def _():
        kmean_ref[...] = jnp.zeros_like(kmean_ref)
        block_lane = lax.broadcasted_iota(jnp.int32, (blk, HEAD_DIM), 1)

        def per_block(j, _):
            ks = pl.multiple_of(j * blk, blk)
            rows = k_ref[pl.ds(ks, blk), :]
            kaug_ref[pl.ds(ks, blk), :HEAD_DIM] = rows
            kaug_ref[pl.ds(ks, blk), HEAD_DIM:] = (block_lane == j).astype(BF16)
            kmean_ref[pl.ds(j, 1), :] = jnp.sum(rows.astype(F32), axis=0, keepdims=True) * (1.0 / blk)
            return 0

        lax.fori_loop(0, nb, per_block, 0)

    tq = MOBA_QBLOCKS * blk
    first_qb = i * MOBA_QBLOCKS
    q = q_ref[...]
    kmean = kmean_ref[...]
    km_hi = kmean.astype(BF16)
    km_lo = (kmean - km_hi.astype(F32)).astype(BF16)
    contract_last = (((1,), (1,)), ((), ()))

    gate_t = (lax.dot_general(km_hi, q, contract_last, preferred_element_type=F32)
              + lax.dot_general(km_lo, q, contract_last, preferred_element_type=F32))[:nb]
    key_blk = lax.broadcasted_iota(jnp.int32, (nb, tq), 0)
    q_col = lax.broadcasted_iota(jnp.int32, (nb, tq), 1)
    q_block_t = first_qb
    for extra in range(1, MOBA_QBLOCKS):
        q_block_t = q_block_t + (q_col >= extra * blk).astype(jnp.int32)
    key_blk_f = key_blk.astype(F32)
    valid = jnp.logical_and(key_blk < q_block_t, gate_t > 0.5 * NEG * q_scale)
    live = jnp.where(valid, gate_t, -jnp.inf)
    selected = jnp.zeros((nb, tq), jnp.bool_)
    for _ in range(MOBA_TOPK):
        top = jnp.max(live, axis=0, keepdims=True)
        hit = jnp.logical_and(live == top, top > -jnp.inf)
        first = jnp.min(jnp.where(hit, key_blk_f, 1e9), axis=0, keepdims=True)
        pick = key_blk_f == first
        selected = jnp.logical_or(selected, pick)
        live = jnp.where(pick, -jnp.inf, live)
    sel_t = jnp.concatenate([selected.astype(F32), jnp.zeros((HEAD_DIM - nb, tq), F32)], axis=0)
    sel = sel_t.T > 0.5
    lane = lax.broadcasted_iota(jnp.int32, (tq, HEAD_DIM), 1)
    row = lax.broadcasted_iota(jnp.int32, (tq, HEAD_DIM), 0)
    q_block = first_qb
    for extra in range(1, MOBA_QBLOCKS):
        q_block = q_block + (row >= extra * blk).astype(jnp.int32)
    bias_past = jnp.where(jnp.logical_and(sel, lane < first_qb), 0.0, NEG).astype(BF16)
    bias_own = jnp.where(jnp.logical_or(sel, lane == q_block), 0.0, NEG).astype(BF16)
    q_past = jnp.concatenate([q, bias_past], axis=1)
    q_own = jnp.concatenate([q, bias_own], axis=1)

    rows_per = MOBA_ROWS
    n_groups = tq // rows_per
    own = pl.multiple_of(first_qb * blk, blk)
    scores = []
    for r in range(n_groups):
        n_own = ((r * rows_per) // blk + 1) * blk
        s = lax.dot_general(q_own[r * rows_per:(r + 1) * rows_per], kaug_ref[pl.ds(own, n_own), :],
                            contract_last, preferred_element_type=F32)
        q_row = lax.broadcasted_iota(jnp.int32, (rows_per, n_own), 0) + r * rows_per
        k_col = lax.broadcasted_iota(jnp.int32, (rows_per, n_own), 1)
        scores.append(jnp.where(k_col <= q_row, s, -jnp.inf))
    soft = []
    for s in scores:
        m = jnp.max(s, axis=1, keepdims=True)
        p = jnp.exp2(s - m)
        soft.append((m, jnp.sum(p, axis=1, keepdims=True), p.astype(BF16)))
    state = []
    for m, l, p in soft:
        acc = jnp.dot(p, v_ref[pl.ds(own, p.shape[1]), :], preferred_element_type=F32)
        state.append((m, l, acc))

    chunk = MOBA_KCHUNK * blk

    def body(c, state):
        ks = pl.multiple_of(c * chunk, chunk)
        k_chunk = kaug_ref[pl.ds(ks, chunk), :]
        v_chunk = v_ref[pl.ds(ks, chunk), :]
        scores = [lax.dot_general(q_past[r * rows_per:(r + 1) * rows_per], k_chunk, contract_last,
                                  preferred_element_type=F32) for r in range(n_groups)]
        soft = []
        for (m, l, acc), s in zip(state, scores):
            m_new = jnp.maximum(m, jnp.max(s, axis=1, keepdims=True))
            alpha = jnp.exp2(m - m_new)
            p = jnp.exp2(s - m_new)
            soft.append((m_new, alpha * l + jnp.sum(p, axis=1, keepdims=True), alpha * acc,
                         p.astype(BF16)))
        return tuple((m, l, acc + jnp.dot(p, v_chunk, preferred_element_type=F32))
                     for m, l, acc, p in soft)

    n_chunks = (first_qb + MOBA_KCHUNK - 1) // MOBA_KCHUNK
    state = lax.fori_loop(0, n_chunks, body, tuple(state))
    for r in range(tq // rows_per):
        _, l, acc = state[r]
        rows = slice(r * rows_per, (r + 1) * rows_per)
        g = g_ref[rows, :].astype(F32)
        o_ref[rows, :] = (acc / l * (g * jax.nn.sigmoid(g))).astype(BF16)


def _moba_attention(proj, batch, seq, q_scale):
    blk = MOBA_QBLOCKS * MOBA_BLOCK
    n_blocks = seq // MOBA_BLOCK
    assert seq % (MOBA_KCHUNK * MOBA_BLOCK) == 0 and n_blocks % 8 == 0 and n_blocks <= HEAD_DIM
    nq = seq // blk
    h = MOBA_HEADS
    kern = functools.partial(_moba_kernel, seq=seq, q_scale=q_scale)
    return pl.pallas_call(
        kern,
        grid=(batch, h, nq),
        in_specs=[
            pl.BlockSpec((blk, HEAD_DIM), lambda b, hh, i: (b * nq + i, hh)),
            pl.BlockSpec((seq, HEAD_DIM), lambda b, hh, i: (b, h + hh)),
            pl.BlockSpec((seq, HEAD_DIM), lambda b, hh, i: (b, 2 * h + hh)),
            pl.BlockSpec((blk, HEAD_DIM), lambda b, hh, i: (b * nq + i, 3 * h + hh)),
        ],
        out_specs=pl.BlockSpec((blk, HEAD_DIM), lambda b, hh, i: (b * nq + i, hh)),
        out_shape=jax.ShapeDtypeStruct((batch * seq, h * HEAD_DIM), BF16),
        scratch_shapes=[pltpu.VMEM((seq, 2 * HEAD_DIM), BF16), pltpu.VMEM((HEAD_DIM, HEAD_DIM), F32)],
        compiler_params=_cparams(3),
        name="moba_attention",
    )(proj, proj, proj, proj)


def _out_proj_kernel(*refs, n_in, alpha):
    a_refs = refs[:n_in]
    w_refs = refs[n_in:2 * n_in]
    x_ref, gate_ref, lng_ref, lnb_ref, o_ref = refs[2 * n_in:]
    y = jnp.dot(a_refs[0][...], w_refs[0][...], preferred_element_type=F32)
    for a_ref, w_ref in zip(a_refs[1:], w_refs[1:]):
        y = y + jnp.dot(a_ref[...], w_ref[...], preferred_element_type=F32)
    z = alpha * x_ref[...] + (1.0 + gate_ref[0]) * y
    mu = jnp.mean(z, axis=-1, keepdims=True)
    zc = z - mu
    var = jnp.mean(zc * zc, axis=-1, keepdims=True)
    o_ref[...] = zc * lax.rsqrt(var + LN_EPS) * lng_ref[...] + lnb_ref[...]


def _out_proj_ln(acts, weights, x2, gate, ln_g, ln_b, seq, alpha):
    t, d = x2.shape
    tm = 512
    per_batch = seq // tm
    n_in = len(acts)
    kern = functools.partial(_out_proj_kernel, n_in=n_in, alpha=alpha)
    in_specs = ([pl.BlockSpec((tm, a.shape[1]), lambda i: (i, 0)) for a in acts]
                + [pl.BlockSpec(w.shape, lambda i: (0, 0)) for w in weights]
                + [pl.BlockSpec((tm, d), lambda i: (i, 0)),
                   pl.BlockSpec((1, 1, d), lambda i: (i // per_batch, 0, 0)),
                   pl.BlockSpec((1, d), lambda i: (0, 0)),
                   pl.BlockSpec((1, d), lambda i: (0, 0))])
    return pl.pallas_call(
        kern,
        grid=(t // tm,),
        in_specs=in_specs,
        out_specs=pl.BlockSpec((tm, d), lambda i: (i, 0)),
        out_shape=jax.ShapeDtypeStruct((t, d), F32),
        compiler_params=_cparams(1),
        name="out_proj_ln",
    )(*acts, *weights, x2, gate, ln_g.reshape(1, d), ln_b.reshape(1, d))


def kernel(x, c, ada_w, ada_b, ln_g, ln_b, even_w_in, even_w_out, ssm_a_re, ssm_a_im, ssm_log_dt,
           ssm_b_re, ssm_b_im, ssm_c_re, ssm_c_im, ssm_d, ssm_w_glu, odd_w_in, odd_w_out):
    batch, seq, d = x.shape
    depth = ada_w.shape[0]
    alpha = (2 * depth) ** 0.25
    q_scale = HEAD_DIM ** -0.5
    sb_width = SB_HEADS * HEAD_DIM

    c_pad = jnp.pad(c, ((0, 8 - batch), (0, 0)))
    mods = _ada_mod(c_pad, ada_w, ada_b)[:, :batch]
    x2 = x.reshape(batch * seq, d)
    for layer in range(depth):
        shift = mods[layer, :, None, :d]
        scale = mods[layer, :, None, d:2 * d]
        gate = mods[layer, :, None, 2 * d:]
        idx = layer // 2
        if layer % 2 == 0:
            proj = _in_proj(x2, scale, shift, even_w_in[idx].astype(BF16), seq,
                            q_width=sb_width, q_scale=q_scale)
            o_sb = _sb_attention(proj, batch, seq)
            ops = _s5_operators(ssm_a_re[idx], ssm_a_im[idx], ssm_log_dt[idx], ssm_b_re[idx],
                                ssm_b_im[idx], ssm_c_re[idx], ssm_c_im[idx])
            o_ssm = _s5_ssm(proj, batch, seq, ops, ssm_d[idx], ssm_w_glu[idx].astype(BF16),
                            4 * sb_width)
            w_out = even_w_out[idx].astype(BF16)
            acts, weights = [o_sb, o_ssm], [w_out[:sb_width], w_out[sb_width:]]
        else:
            q_scale2 = q_scale * math.log2(math.e)
            proj = _in_proj(x2, scale, shift, odd_w_in[idx].astype(BF16), seq,
                            q_width=MOBA_HEADS * HEAD_DIM, q_scale=q_scale2)
            o = _moba_attention(proj, batch, seq, q_scale2)
            acts, weights = [o], [odd_w_out[idx].astype(BF16)]
        x2 = _out_proj_ln(acts, weights, x2, gate, ln_g[layer], ln_b[layer], seq, alpha)
    return x2.reshape(batch, seq, d)
```

```python
import functools
import math

import jax
import jax.numpy as jnp
from jax import lax
from jax.experimental import pallas as pl
from jax.experimental.pallas import tpu as pltpu

F32 = jnp.float32
BF16 = jnp.bfloat16

HEAD_DIM = 128
SB_HEADS = 12
SSM_WIDTH = 512
SSM_GROUP = 16
SSM_GROUPS = 32
SSM_STATE = 64
MOBA_HEADS = 16
MOBA_BLOCK = 256
MOBA_TOPK = 3
LN_EPS = 1e-5
NEG = -1e30

SSM_CHUNK = 16
SSM_FLAT = SSM_CHUNK * SSM_GROUP
SB_TILE = 256
SB_QTILES = 4
MOBA_QBLOCKS = 4
MOBA_KCHUNK = 4
MOBA_ROWS = 256
SB_UNDERFLOW = 105.0

VMEM_LIMIT = 56 * 1024 * 1024


def _cparams(n_axes):
    return pltpu.CompilerParams(
        dimension_semantics=("arbitrary",) * n_axes, vmem_limit_bytes=VMEM_LIMIT)


def _ada_kernel(c_ref, w_ref, b_ref, o_ref):
    c = c_ref[...]
    cond = c * jax.nn.sigmoid(c)
    o_ref[0] = jnp.dot(cond, w_ref[0], preferred_element_type=F32,
                       precision=lax.Precision.HIGHEST) + b_ref[0]


def _ada_mod(c_pad, ada_w, ada_b):
    depth, d, n = ada_w.shape
    rows = c_pad.shape[0]
    tn = 1024
    return pl.pallas_call(
        _ada_kernel,
        grid=(depth, n // tn),
        in_specs=[
            pl.BlockSpec((rows, d), lambda l, j: (0, 0)),
            pl.BlockSpec((1, d, tn), lambda l, j: (l, 0, j)),
            pl.BlockSpec((1, 1, tn), lambda l, j: (l, 0, j)),
        ],
        out_specs=pl.BlockSpec((1, rows, tn), lambda l, j: (l, 0, j)),
        out_shape=jax.ShapeDtypeStruct((depth, rows, n), F32),
        compiler_params=_cparams(2),
        name="ada_mod",
    )(c_pad, ada_w, ada_b.reshape(depth, 1, n))


def _in_proj_kernel(x_ref, sc_ref, sh_ref, w_ref, o_ref, h_ref, *, q_width, q_scale):
    j = pl.program_id(1)
    tn = o_ref.shape[1]

    @pl.when(j == 0)
    def _():
        h = x_ref[...] * (1.0 + sc_ref[0]) + sh_ref[0]
        h_ref[...] = h.astype(BF16)

    acc = jnp.dot(h_ref[...], w_ref[...], preferred_element_type=F32)
    column = j * tn + lax.broadcasted_iota(jnp.int32, (1, tn), 1)
    o_ref[...] = (acc * jnp.where(column < q_width, q_scale, 1.0)).astype(BF16)


def _in_proj(x2, scale, shift, w_bf16, seq, *, q_width, q_scale):
    t, d = x2.shape
    n = w_bf16.shape[1]
    tm, tn = 1024, 1024
    per_batch = seq // tm
    kern = functools.partial(_in_proj_kernel, q_width=q_width, q_scale=q_scale)
    return pl.pallas_call(
        kern,
        grid=(t // tm, n // tn),
        in_specs=[
            pl.BlockSpec((tm, d), lambda i, j: (i, 0)),
            pl.BlockSpec((1, 1, d), lambda i, j: (i // per_batch, 0, 0)),
            pl.BlockSpec((1, 1, d), lambda i, j: (i // per_batch, 0, 0)),
            pl.BlockSpec((d, tn), lambda i, j: (0, j)),
        ],
        out_specs=pl.BlockSpec((tm, tn), lambda i, j: (i, j)),
        out_shape=jax.ShapeDtypeStruct((t, n), BF16),
        scratch_shapes=[pltpu.VMEM((tm, d), BF16)],
        compiler_params=_cparams(2),
        name="in_proj",
    )(x2, scale, shift, w_bf16)


def _sb_kernel(q_ref, k_ref, v_ref, g_ref, o_ref, *, tile):
    i = pl.program_id(2)
    n_groups = SB_QTILES
    qs = [q_ref[g * tile:(g + 1) * tile, :] for g in range(n_groups)]
    row = lax.broadcasted_iota(jnp.int32, (tile, tile), 0)
    col = lax.broadcasted_iota(jnp.int32, (tile, tile), 1)
    later_key = (row > col).astype(BF16)

    def sweep(jobs, carries):
        starts = [pl.multiple_of(blk * tile, tile) for _, blk, _ in jobs]
        zs = [lax.dot_general(qs[g], k_ref[pl.ds(ks, tile), :], (((1,), (1,)), ((), ())),
                              preferred_element_type=F32) for (g, _, _), ks in zip(jobs, starts)]
        softplus = []
        for z, (_, _, past) in zip(zs, jobs):
            sp = jnp.maximum(z, 0.0) + jnp.log(1.0 + jnp.exp(-jnp.abs(z)))
            softplus.append(sp if past is None else jnp.where(past, sp, 0.0))
        within = [jnp.dot(sp.astype(BF16), later_key, preferred_element_type=F32) for sp in softplus]
        carries = list(carries)
        weights = []
        for z, sp, inner, (g, _, past) in zip(zs, softplus, within, jobs):
            w = jnp.exp(z - sp - inner - carries[g])
            weights.append((w if past is None else jnp.where(past, w, 0.0)).astype(BF16))
            carries[g] = carries[g] + jnp.sum(sp, axis=1, keepdims=True)
        outs = [None] * n_groups
        for w, ks, (g, _, _) in zip(weights, starts, jobs):
            part = jnp.dot(w, v_ref[pl.ds(ks, tile), :], preferred_element_type=F32)
            outs[g] = part if outs[g] is None else outs[g] + part
        return tuple(carries), tuple(outs)

    def exists(blk):
        return jnp.broadcast_to(blk >= 0, (tile, tile))

    first = i * n_groups
    jobs = [(g, first + g, col < row) for g in range(n_groups)]
    jobs.append((0, jnp.maximum(first - 1, 0), exists(first - 1)))
    jobs += [(g, first + g - 1, None) for g in range(1, n_groups)]
    carries, accs = sweep(jobs, (jnp.zeros((tile, 1), F32),) * n_groups)

    def older(t, g):
        return first + g - 2 - t

    def cond(state):
        t, carries, _ = state
        more = False
        for g in range(n_groups):
            live = jnp.logical_and(older(t, g) >= 0, jnp.min(carries[g]) < SB_UNDERFLOW)
            more = jnp.logical_or(more, live)
        return more

    def body(state):
        t, carries, accs = state
        jobs = [(g, jnp.maximum(older(t, g), 0), exists(older(t, g))) for g in range(n_groups)]
        carries, outs = sweep(jobs, carries)
        return t + 1, carries, tuple(a + o for a, o in zip(accs, outs))

    _, _, accs = lax.while_loop(cond, body, (0, carries, accs))
    for g in range(n_groups):
        gate = g_ref[g * tile:(g + 1) * tile, :].astype(F32)
        o_ref[g * tile:(g + 1) * tile, :] = (accs[g] * (gate * jax.nn.sigmoid(gate))).astype(BF16)


def _sb_attention(proj, batch, seq):
    tile = SB_TILE
    rows = SB_QTILES * tile
    nq = seq // rows
    h = SB_HEADS
    kern = functools.partial(_sb_kernel, tile=tile)
    return pl.pallas_call(
        kern,
        grid=(batch, h, nq),
        in_specs=[
            pl.BlockSpec((rows, HEAD_DIM), lambda b, hh, i: (b * nq + i, hh)),
            pl.BlockSpec((seq, HEAD_DIM), lambda b, hh, i: (b, h + hh)),
            pl.BlockSpec((seq, HEAD_DIM), lambda b, hh, i: (b, 2 * h + hh)),
            pl.BlockSpec((rows, HEAD_DIM), lambda b, hh, i: (b * nq + i, 3 * h + hh)),
        ],
        out_specs=pl.BlockSpec((rows, HEAD_DIM), lambda b, hh, i: (b * nq + i, hh)),
        out_shape=jax.ShapeDtypeStruct((batch * seq, h * HEAD_DIM), BF16),
        compiler_params=_cparams(3),
        name="sb_attention",
    )(proj, proj, proj, proj)


def _s5_local_kernel(u_ref, mre_ref, mim_ref, sre_ref, sim_ref):
    parts_re, parts_im = [], []
    for g in range(2):
        u = u_ref[g]
        parts_re.append(jnp.dot(u, mre_ref[g], preferred_element_type=F32))
        parts_im.append(jnp.dot(u, mim_ref[g], preferred_element_type=F32))
    sre_ref[...] = jnp.concatenate(parts_re, axis=1)
    sim_ref[...] = jnp.concatenate(parts_im, axis=1)


def _s5_local_state(u_flat, m_in_re, m_in_im):
    groups, rows, flat = u_flat.shape
    p = m_in_re.shape[2]
    return pl.pallas_call(
        _s5_local_kernel,
        grid=(groups // 2,),
        in_specs=[
            pl.BlockSpec((2, rows, flat), lambda g: (g, 0, 0)),
            pl.BlockSpec((2, flat, p), lambda g: (g, 0, 0)),
            pl.BlockSpec((2, flat, p), lambda g: (g, 0, 0)),
        ],
        out_specs=[
            pl.BlockSpec((rows, 2 * p), lambda g: (0, g)),
            pl.BlockSpec((rows, 2 * p), lambda g: (0, g)),
        ],
        out_shape=[jax.ShapeDtypeStruct((rows, groups * p), F32)] * 2,
        compiler_params=_cparams(1),
        name="s5_local_state",
    )(u_flat, m_in_re, m_in_im)


def _s5_scan_kernel(sre_ref, sim_ref, are_ref, aim_ref, ore_ref, oim_ref):
    n_chunks, batch, width = sre_ref.shape
    a_re = are_ref[...]
    a_im = aim_ref[...]

    def body(c, state):
        st_re, st_im = state
        ore_ref[c] = st_re
        oim_ref[c] = st_im
        new_re = a_re * st_re - a_im * st_im + sre_ref[c]
        new_im = a_re * st_im + a_im * st_re + sim_ref[c]
        return new_re, new_im

    zero = jnp.zeros((batch, width), F32)
    lax.fori_loop(0, n_chunks, body, (zero, zero))


def _s5_scan(s_re, s_im, a_re, a_im, batch):
    rows, width = s_re.shape
    n_chunks = rows // batch
    wt = 128
    blk = pl.BlockSpec((n_chunks, batch, wt), lambda j: (0, 0, j))
    vec = pl.BlockSpec((1, wt), lambda j: (0, j))
    prev_re, prev_im = pl.pallas_call(
        _s5_scan_kernel,
        grid=(width // wt,),
        in_specs=[blk, blk, vec, vec],
        out_specs=[blk, blk],
        out_shape=[jax.ShapeDtypeStruct((n_chunks, batch, width), F32)] * 2,
        compiler_params=_cparams(1),
        name="s5_scan",
    )(s_re.reshape(n_chunks, batch, width), s_im.reshape(n_chunks, batch, width), a_re, a_im)
    return prev_re.reshape(rows, width), prev_im.reshape(rows, width)


def _s5_output_kernel(u_ref, t_ref, pre_ref, pim_ref, ore_ref, oim_ref, y_ref):
    flat = t_ref.shape[1]
    carried = (jnp.dot(pre_ref[...].astype(BF16), ore_ref[0], preferred_element_type=F32)
               + jnp.dot(pim_ref[...].astype(BF16), oim_ref[0], preferred_element_type=F32))
    for g in range(2):
        cols = slice(g * flat, (g + 1) * flat)
        local = jnp.dot(u_ref[g], t_ref[g], preferred_element_type=F32)
        y_ref[0, :, cols] = (local + carried[:, cols]).astype(BF16)


def _s5_output(u_flat, toeplitz, prev_re, prev_im, m_out_re, m_out_im):
    groups, rows, flat = u_flat.shape
    pp = m_out_re.shape[1]
    return pl.pallas_call(
        _s5_output_kernel,
        grid=(groups // 2,),
        in_specs=[
            pl.BlockSpec((2, rows, flat), lambda g: (g, 0, 0)),
            pl.BlockSpec((2, flat, flat), lambda g: (g, 0, 0)),
            pl.BlockSpec((rows, pp), lambda g: (0, g)),
            pl.BlockSpec((rows, pp), lambda g: (0, g)),
            pl.BlockSpec((1, pp, 2 * flat), lambda g: (g, 0, 0)),
            pl.BlockSpec((1, pp, 2 * flat), lambda g: (g, 0, 0)),
        ],
        out_specs=pl.BlockSpec((1, rows, 2 * flat), lambda g: (g, 0, 0)),
        out_shape=jax.ShapeDtypeStruct((groups // 2, rows, 2 * flat), BF16),
        compiler_params=_cparams(1),
        name="s5_output",
    )(u_flat, toeplitz, prev_re, prev_im, m_out_re, m_out_im)


def _s5_epilogue_kernel(y_ref, u_ref, g_ref, d_ref, w_ref, o_ref):
    y = y_ref[...].astype(F32) + d_ref[...] * u_ref[...].astype(F32)
    y = jax.nn.gelu(y)
    gate = jnp.dot(y.astype(BF16), w_ref[...], preferred_element_type=F32)
    y = y * jax.nn.sigmoid(gate)
    g = g_ref[...].astype(F32)
    o_ref[...] = (y * (g * jax.nn.sigmoid(g))).astype(BF16)


def _s5_epilogue(y, proj, d_skip, w_glu_bf16, u_col, g_col):
    t, w = y.shape
    tm = 1024
    return pl.pallas_call(
        _s5_epilogue_kernel,
        grid=(t // tm,),
        in_specs=[
            pl.BlockSpec((tm, w), lambda i: (i, 0)),
            pl.BlockSpec((tm, w), lambda i: (i, u_col)),
            pl.BlockSpec((tm, w), lambda i: (i, g_col)),
            pl.BlockSpec((1, w), lambda i: (0, 0)),
            pl.BlockSpec((w, w), lambda i: (0, 0)),
        ],
        out_specs=pl.BlockSpec((tm, w), lambda i: (i, 0)),
        out_shape=jax.ShapeDtypeStruct((t, w), BF16),
        compiler_params=_cparams(1),
        name="s5_epilogue",
    )(y, proj, proj, d_skip.reshape(1, w), w_glu_bf16)


def _s5_operators(a_re, a_im, log_dt, b_re, b_im, c_re, c_im):
    hi = lax.Precision.HIGHEST
    length = SSM_CHUNK
    lam_re, lam_im = a_re.astype(F32), a_im.astype(F32)
    dt = jnp.exp(log_dt.astype(F32))[:, None]
    steps = jnp.arange(length + 1, dtype=F32)[None, :, None]
    mag = jnp.exp((lam_re * dt)[:, None, :] * steps)
    ang = (lam_im * dt)[:, None, :] * steps
    pw_re, pw_im = mag * jnp.cos(ang), mag * jnp.sin(ang)
    num_re, num_im = pw_re[:, 1] - 1.0, pw_im[:, 1]
    den = lam_re * lam_re + lam_im * lam_im
    quo_re = (num_re * lam_re + num_im * lam_im) / den
    quo_im = (num_im * lam_re - num_re * lam_im) / den
    b_re, b_im = b_re.astype(F32), b_im.astype(F32)
    bb_re = quo_re[..., None] * b_re - quo_im[..., None] * b_im
    bb_im = quo_re[..., None] * b_im + quo_im[..., None] * b_re
    c_re, c_im = c_re.astype(F32)[:, None], c_im.astype(F32)[:, None]
    cp_re = c_re * pw_re[:, :, None, :] - c_im * pw_im[:, :, None, :]
    cp_im = c_re * pw_im[:, :, None, :] + c_im * pw_re[:, :, None, :]

    lag_kernel = (jnp.einsum('gnip,gpj->gjni', cp_re[:, :length], bb_re, precision=hi)
                  - jnp.einsum('gnip,gpj->gjni', cp_im[:, :length], bb_im, precision=hi)).astype(BF16)
    period = jnp.concatenate([lag_kernel, jnp.zeros_like(lag_kernel), jnp.zeros_like(lag_kernel[:, :, :1])],
                             axis=2)
    skew = jnp.tile(period, (1, 1, length, 1))[:, :, :2 * length * length]
    skew = skew.reshape(SSM_GROUPS, SSM_GROUP, length, 2 * length, SSM_GROUP)[:, :, :, :length]
    toeplitz = skew.transpose(0, 2, 1, 3, 4).reshape(SSM_GROUPS, SSM_FLAT, SSM_FLAT)

    rev = length - 1 - jnp.arange(length)
    in_re = pw_re[:, rev, :, None] * bb_re[:, None] - pw_im[:, rev, :, None] * bb_im[:, None]
    in_im = pw_re[:, rev, :, None] * bb_im[:, None] + pw_im[:, rev, :, None] * bb_re[:, None]

    def flat_in(m):
        return m.transpose(0, 1, 3, 2).reshape(SSM_GROUPS, SSM_FLAT, SSM_STATE).astype(BF16)

    def flat_out(m):
        m = m.transpose(0, 3, 1, 2).reshape(SSM_GROUPS // 2, 2, SSM_STATE, SSM_FLAT)
        z = jnp.zeros_like(m[:, 0])
        top = jnp.concatenate([m[:, 0], z], axis=2)
        bot = jnp.concatenate([z, m[:, 1]], axis=2)
        return jnp.concatenate([top, bot], axis=1).astype(BF16)

    return dict(
        toeplitz=toeplitz,
        m_in_re=flat_in(in_re), m_in_im=flat_in(in_im),
        m_out_re=flat_out(cp_re[:, 1:]), m_out_im=flat_out(-cp_im[:, 1:]),
        a_re=pw_re[:, length].reshape(1, -1), a_im=pw_im[:, length].reshape(1, -1),
    )


def _s5_ssm(proj, batch, seq, ops, d_skip, w_glu_bf16, u_col0):
    t = batch * seq
    n_chunks = seq // SSM_CHUNK
    n_rows = batch * n_chunks
    u = proj[:, u_col0:u_col0 + SSM_WIDTH]
    u_flat = (u.reshape(batch, n_chunks, SSM_CHUNK, SSM_GROUPS, SSM_GROUP)
              .transpose(3, 1, 0, 2, 4).reshape(SSM_GROUPS, n_rows, SSM_FLAT))
    s_re, s_im = _s5_local_state(u_flat, ops['m_in_re'], ops['m_in_im'])
    prev_re, prev_im = _s5_scan(s_re, s_im, ops['a_re'], ops['a_im'], batch)
    y_flat = _s5_output(u_flat, ops['toeplitz'], prev_re, prev_im, ops['m_out_re'], ops['m_out_im'])
    y = (y_flat.reshape(SSM_GROUPS // 2, n_chunks, batch, 2, SSM_CHUNK, SSM_GROUP)
         .transpose(2, 1, 4, 0, 3, 5).reshape(t, SSM_WIDTH))
    return _s5_epilogue(y, proj, d_skip, w_glu_bf16, u_col0 // SSM_WIDTH, u_col0 // SSM_WIDTH + 1)


def _moba_kernel(q_ref, k_ref, v_ref, g_ref, o_ref, kaug_ref, kmean_ref, *, seq, q_scale):
    blk = MOBA_BLOCK
    nb = seq // blk
    i = pl.program_id(2)

    @pl.when(i == 0)
    def _():
        kmean_ref[...] = jnp.zeros_like(kmean_ref)
        block_lane = lax.broadcasted_iota(jnp.int32, (blk, HEAD_DIM), 1)

        def per_block(j, _):
            ks = pl.multiple_of(j * blk, blk)
            rows = k_ref[pl.ds(ks, blk), :]
            kaug_ref[pl.ds(ks, blk), :HEAD_DIM] = rows
            kaug_ref[pl.ds(ks, blk), HEAD_DIM:] = (block_lane == j).astype(BF16)
            kmean_ref[pl.ds(j, 1), :] = jnp.sum(rows.astype(F32), axis=0, keepdims=True) * (1.0 / blk)
            return 0

        lax.fori_loop(0, nb, per_block, 0)

    tq = MOBA_QBLOCKS * blk
    first_qb = i * MOBA_QBLOCKS
    q = q_ref[...]
    kmean = kmean_ref[...]
    km_hi = kmean.astype(BF16)
    km_lo = (kmean - km_hi.astype(F32)).astype(BF16)
    contract_last = (((1,), (1,)), ((), ()))

    gate_t = (lax.dot_general(km_hi, q, contract_last, preferred_element_type=F32)
              + lax.dot_general(km_lo, q, contract_last, preferred_element_type=F32))[:nb]
    key_blk = lax.broadcasted_iota(jnp.int32, (nb, tq), 0)
    q_col = lax.broadcasted_iota(jnp.int32, (nb, tq), 1)
    q_block_t = first_qb
    for extra in range(1, MOBA_QBLOCKS):
        q_block_t = q_block_t + (q_col >= extra * blk).astype(jnp.int32)
    key_blk_f = key_blk.astype(F32)
    valid = jnp.logical_and(key_blk < q_block_t, gate_t > 0.5 * NEG * q_scale)
    live = jnp.where(valid, gate_t, -jnp.inf)
    selected = jnp.zeros((nb, tq), jnp.bool_)
    for _ in range(MOBA_TOPK):
        top = jnp.max(live, axis=0, keepdims=True)
        hit = jnp.logical_and(live == top, top > -jnp.inf)
        first = jnp.min(jnp.where(hit, key_blk_f, 1e9), axis=0, keepdims=True)
        pick = key_blk_f == first
        selected = jnp.logical_or(selected, pick)
        live = jnp.where(pick, -jnp.inf, live)
    sel_t = jnp.concatenate([selected.astype(F32), jnp.zeros((HEAD_DIM - nb, tq), F32)], axis=0)
    sel = sel_t.T > 0.5
    lane = lax.broadcasted_iota(jnp.int32, (tq, HEAD_DIM), 1)
    row = lax.broadcasted_iota(jnp.int32, (tq, HEAD_DIM), 0)
    q_block = first_qb
    for extra in range(1, MOBA_QBLOCKS):
        q_block = q_block + (row >= extra * blk).astype(jnp.int32)
    bias_past = jnp.where(jnp.logical_and(sel, lane < first_qb), 0.0, NEG).astype(BF16)
    bias_own = jnp.where(jnp.logical_or(sel, lane == q_block), 0.0, NEG).astype(BF16)
    q_past = jnp.concatenate([q, bias_past], axis=1)
    q_own = jnp.concatenate([q, bias_own], axis=1)

    rows_per = MOBA_ROWS
    n_groups = tq // rows_per
    own = pl.multiple_of(first_qb * blk, blk)
    scores = []
    for r in range(n_groups):
        n_own = ((r * rows_per) // blk + 1) * blk
        s = lax.dot_general(q_own[r * rows_per:(r + 1) * rows_per], kaug_ref[pl.ds(own, n_own), :],
                            contract_last, preferred_element_type=F32)
        q_row = lax.broadcasted_iota(jnp.int32, (rows_per, n_own), 0) + r * rows_per
        k_col = lax.broadcasted_iota(jnp.int32, (rows_per, n_own), 1)
        scores.append(jnp.where(k_col <= q_row, s, -jnp.inf))
    soft = []
    for s in scores:
        m = jnp.max(s, axis=1, keepdims=True)
        p = jnp.exp2(s - m)
        soft.append((m, jnp.sum(p, axis=1, keepdims=True), p.astype(BF16)))
    state = []
    for m, l, p in soft:
        acc = jnp.dot(p, v_ref[pl.ds(own, p.shape[1]), :], preferred_element_type=F32)
        state.append((m, l, acc))

    chunk = MOBA_KCHUNK * blk

    def body(c, state):
        ks = pl.multiple_of(c * chunk, chunk)
        k_chunk = kaug_ref[pl.ds(ks, chunk), :]
        v_chunk = v_ref[pl.ds(ks, chunk), :]
        scores = [lax.dot_general(q_past[r * rows_per:(r + 1) * rows_per], k_chunk, contract_last,
                                  preferred_element_type=F32) for r in range(n_groups)]
        soft = []
        for (m, l, acc), s in zip(state, scores):
            m_new = jnp.maximum(m, jnp.max(s, axis=1, keepdims=True))
            alpha = jnp.exp2(m - m_new)
            p = jnp.exp2(s - m_new)
            soft.append((m_new, alpha * l + jnp.sum(p, axis=1, keepdims=True), alpha * acc,
                         p.astype(BF16)))
        return tuple((m, l, acc + jnp.dot(p, v_chunk, preferred_element_type=F32))
                     for m, l, acc, p in soft)

    n_chunks = (first_qb + MOBA_KCHUNK - 1) // MOBA_KCHUNK
    state = lax.fori_loop(0, n_chunks, body, tuple(state))
    for r in range(tq // rows_per):
        _, l, acc = state[r]
        rows = slice(r * rows_per, (r + 1) * rows_per)
        g = g_ref[rows, :].astype(F32)
        o_ref[rows, :] = (acc / l * (g * jax.nn.sigmoid(g))).astype(BF16)


def _moba_attention(proj, batch, seq, q_scale):
    blk = MOBA_QBLOCKS * MOBA_BLOCK
    n_blocks = seq // MOBA_BLOCK
    assert seq % (MOBA_KCHUNK * MOBA_BLOCK) == 0 and n_blocks % 8 == 0 and n_blocks <= HEAD_DIM
    nq = seq // blk
    h = MOBA_HEADS
    kern = functools.partial(_moba_kernel, seq=seq, q_scale=q_scale)
    return pl.pallas_call(
        kern,
        grid=(batch, h, nq),
        in_specs=[
            pl.BlockSpec((blk, HEAD_DIM), lambda b, hh, i: (b * nq + i, hh)),
            pl.BlockSpec((seq, HEAD_DIM), lambda b, hh, i: (b, h + hh)),
            pl.BlockSpec((seq, HEAD_DIM), lambda b, hh, i: (b, 2 * h + hh)),
            pl.BlockSpec((blk, HEAD_DIM), lambda b, hh, i: (b * nq + i, 3 * h + hh)),
        ],
        out_specs=pl.BlockSpec((blk, HEAD_DIM), lambda b, hh, i: (b * nq + i, hh)),
        out_shape=jax.ShapeDtypeStruct((batch * seq, h * HEAD_DIM), BF16),
        scratch_shapes=[pltpu.VMEM((seq, 2 * HEAD_DIM), BF16), pltpu.VMEM((HEAD_DIM, HEAD_DIM), F32)],
        compiler_params=_cparams(3),
        name="moba_attention",
    )(proj, proj, proj, proj)


def _out_proj_kernel(*refs, n_in, alpha):
    a_refs = refs[:n_in]
    w_refs = refs[n_in:2 * n_in]
    x_ref, gate_ref, lng_ref, lnb_ref, o_ref = refs[2 * n_in:]
    y = jnp.dot(a_refs[0][...], w_refs[0][...], preferred_element_type=F32)
    for a_ref, w_ref in zip(a_refs[1:], w_refs[1:]):
        y = y + jnp.dot(a_ref[...], w_ref[...], preferred_element_type=F32)
    z = alpha * x_ref[...] + (1.0 + gate_ref[0]) * y
    mu = jnp.mean(z, axis=-1, keepdims=True)
    zc = z - mu
    var = jnp.mean(zc * zc, axis=-1, keepdims=True)
    o_ref[...] = zc * lax.rsqrt(var + LN_EPS) * lng_ref[...] + lnb_ref[...]


def _out_proj_ln(acts, weights, x2, gate, ln_g, ln_b, seq, alpha):
    t, d = x2.shape
    tm = 512
    per_batch = seq // tm
    n_in = len(acts)
    kern = functools.partial(_out_proj_kernel, n_in=n_in, alpha=alpha)
    in_specs = ([pl.BlockSpec((tm, a.shape[1]), lambda i: (i, 0)) for a in acts]
                + [pl.BlockSpec(w.shape, lambda i: (0, 0)) for w in weights]
                + [pl.BlockSpec((tm, d), lambda i: (i, 0)),
                   pl.BlockSpec((1, 1, d), lambda i: (i // per_batch, 0, 0)),
                   pl.BlockSpec((1, d), lambda i: (0, 0)),
                   pl.BlockSpec((1, d), lambda i: (0, 0))])
    return pl.pallas_call(
        kern,
        grid=(t // tm,),
        in_specs=in_specs,
        out_specs=pl.BlockSpec((tm, d), lambda i: (i, 0)),
        out_shape=jax.ShapeDtypeStruct((t, d), F32),
        compiler_params=_cparams(1),
        name="out_proj_ln",
    )(*acts, *weights, x2, gate, ln_g.reshape(1, d), ln_b.reshape(1, d))


def kernel(x, c, ada_w, ada_b, ln_g, ln_b, even_w_in, even_w_out, ssm_a_re, ssm_a_im, ssm_log_dt,
           ssm_b_re, ssm_b_im, ssm_c_re, ssm_c_im, ssm_d, ssm_w_glu, odd_w_in, odd_w_out):
    batch, seq, d = x.shape
    depth = ada_w.shape[0]
    alpha = (2 * depth) ** 0.25
    q_scale = HEAD_DIM ** -0.5
    sb_width = SB_HEADS * HEAD_DIM

    c_pad = jnp.pad(c, ((0, 8 - batch), (0, 0)))
    mods = _ada_mod(c_pad, ada_w, ada_b)[:, :batch]
    x2 = x.reshape(batch * seq, d)
    for layer in range(depth):
        shift = mods[layer, :, None, :d]
        scale = mods[layer, :, None, d:2 * d]
        gate = mods[layer, :, None, 2 * d:]
        idx = layer // 2
        if layer % 2 == 0:
            proj = _in_proj(x2, scale, shift, even_w_in[idx].astype(BF16), seq,
                            q_width=sb_width, q_scale=q_scale)
            o_sb = _sb_attention(proj, batch, seq)
            ops = _s5_operators(ssm_a_re[idx], ssm_a_im[idx], ssm_log_dt[idx], ssm_b_re[idx],
                                ssm_b_im[idx], ssm_c_re[idx], ssm_c_im[idx])
            o_ssm = _s5_ssm(proj, batch, seq, ops, ssm_d[idx], ssm_w_glu[idx].astype(BF16),
                            4 * sb_width)
            w_out = even_w_out[idx].astype(BF16)
            acts, weights = [o_sb, o_ssm], [w_out[:sb_width], w_out[sb_width:]]
        else:
            q_scale2 = q_scale * math.log2(math.e)
            proj = _in_proj(x2, scale, shift, odd_w_in[idx].astype(BF16), seq,
                            q_width=MOBA_HEADS * HEAD_DIM, q_scale=q_scale2)
            o = _moba_attention(proj, batch, seq, q_scale2)
            acts, weights = [o], [odd_w_out[idx].astype(BF16)]
        x2 = _out_proj_ln(acts, weights, x2, gate, ln_g[layer], ln_b[layer], seq, alpha)
    return x2.reshape(batch, seq, d)
```

```python
import functools
import math

import jax
import jax.numpy as jnp
from jax import lax
from jax.experimental import pallas as pl
from jax.experimental.pallas import tpu as pltpu

F32 = jnp.float32
BF16 = jnp.bfloat16

HEAD_DIM = 128
SB_HEADS = 12
SSM_WIDTH = 512
SSM_GROUP = 16
SSM_GROUPS = 32
SSM_STATE = 64
MOBA_HEADS = 16
MOBA_BLOCK = 256
MOBA_TOPK = 3
LN_EPS = 1e-5
NEG = -1e30

SSM_CHUNK = 16
SSM_FLAT = SSM_CHUNK * SSM_GROUP
SB_TILE = 256
SB_QTILES = 4
MOBA_QBLOCKS = 4
MOBA_KCHUNK = 4
MOBA_ROWS = 256
SB_UNDERFLOW = 105.0

VMEM_LIMIT = 56 * 1024 * 1024
SUBLANES = 8


def _cparams(n_axes):
    return pltpu.CompilerParams(
        dimension_semantics=("arbitrary",) * n_axes, vmem_limit_bytes=VMEM_LIMIT)


def _ada_kernel(c_ref, w_ref, b_ref, o_ref):
    c = c_ref[...]
    cond = c * jax.nn.sigmoid(c)
    o_ref[0] = jnp.dot(cond, w_ref[0], preferred_element_type=F32,
                       precision=lax.Precision.HIGHEST) + b_ref[0]


def _ada_mod(c_pad, ada_w, ada_b):
    depth, d, n = ada_w.shape
    rows = c_pad.shape[0]
    tn = 1024
    return pl.pallas_call(
        _ada_kernel,
        grid=(depth, n // tn),
        in_specs=[
            pl.BlockSpec((rows, d), lambda l, j: (0, 0)),
            pl.BlockSpec((1, d, tn), lambda l, j: (l, 0, j)),
            pl.BlockSpec((1, 1, tn), lambda l, j: (l, 0, j)),
        ],
        out_specs=pl.BlockSpec((1, rows, tn), lambda l, j: (l, 0, j)),
        out_shape=jax.ShapeDtypeStruct((depth, rows, n), F32),
        compiler_params=_cparams(2),
        name="ada_mod",
    )(c_pad, ada_w, ada_b.reshape(depth, 1, n))


def _in_proj_kernel(x_ref, sc_ref, sh_ref, w_ref, o_ref, h_ref, *, q_width, q_scale):
    j = pl.program_id(1)
    tn = o_ref.shape[1]

    @pl.when(j == 0)
    def _():
        h = x_ref[...] * (1.0 + sc_ref[0]) + sh_ref[0]
        h_ref[...] = h.astype(BF16)

    acc = jnp.dot(h_ref[...], w_ref[...], preferred_element_type=F32)
    column = j * tn + lax.broadcasted_iota(jnp.int32, (1, tn), 1)
    o_ref[...] = (acc * jnp.where(column < q_width, q_scale, 1.0)).astype(BF16)


def _in_proj(x2, scale, shift, w_bf16, seq, *, q_width, q_scale):
    t, d = x2.shape
    n = w_bf16.shape[1]
    tm, tn = 1024, 1024
    per_batch = seq // tm
    kern = functools.partial(_in_proj_kernel, q_width=q_width, q_scale=q_scale)
    return pl.pallas_call(
        kern,
        grid=(t // tm, n // tn),
        in_specs=[
            pl.BlockSpec((tm, d), lambda i, j: (i, 0)),
            pl.BlockSpec((1, 1, d), lambda i, j: (i // per_batch, 0, 0)),
            pl.BlockSpec((1, 1, d), lambda i, j: (i // per_batch, 0, 0)),
            pl.BlockSpec((d, tn), lambda i, j: (0, j)),
        ],
        out_specs=pl.BlockSpec((tm, tn), lambda i, j: (i, j)),
        out_shape=jax.ShapeDtypeStruct((t, n), BF16),
        scratch_shapes=[pltpu.VMEM((tm, d), BF16)],
        compiler_params=_cparams(2),
        name="in_proj",
    )(x2, scale, shift, w_bf16)


def _sb_kernel(q_ref, k_ref, v_ref, g_ref, o_ref, *, tile):
    i = pl.program_id(2)
    n_groups = SB_QTILES
    qs = [q_ref[g * tile:(g + 1) * tile, :] for g in range(n_groups)]
    row = lax.broadcasted_iota(jnp.int32, (tile, tile), 0)
    col = lax.broadcasted_iota(jnp.int32, (tile, tile), 1)
    later_key = (row > col).astype(BF16)

    def sweep(jobs, carries):
        starts = [pl.multiple_of(blk * tile, tile) for _, blk, _ in jobs]
        zs = [lax.dot_general(qs[g], k_ref[pl.ds(ks, tile), :], (((1,), (1,)), ((), ())),
                              preferred_element_type=F32) for (g, _, _), ks in zip(jobs, starts)]
        softplus = []
        for z, (_, _, past) in zip(zs, jobs):
            sp = jnp.maximum(z, 0.0) + jnp.log(1.0 + jnp.exp(-jnp.abs(z)))
            softplus.append(sp if past is None else jnp.where(past, sp, 0.0))
        within = [jnp.dot(sp.astype(BF16), later_key, preferred_element_type=F32) for sp in softplus]
        carries = list(carries)
        weights = []
        for z, sp, inner, (g, _, past) in zip(zs, softplus, within, jobs):
            w = jnp.exp(z - sp - inner - carries[g])
            weights.append((w if past is None else jnp.where(past, w, 0.0)).astype(BF16))
            carries[g] = carries[g] + jnp.sum(sp, axis=1, keepdims=True)
        outs = [None] * n_groups
        for w, ks, (g, _, _) in zip(weights, starts, jobs):
            part = jnp.dot(w, v_ref[pl.ds(ks, tile), :], preferred_element_type=F32)
            outs[g] = part if outs[g] is None else outs[g] + part
        return tuple(carries), tuple(outs)

    def exists(blk):
        return jnp.broadcast_to(blk >= 0, (tile, tile))

    first = i * n_groups
    jobs = [(g, first + g, col < row) for g in range(n_groups)]
    jobs.append((0, jnp.maximum(first - 1, 0), exists(first - 1)))
    jobs += [(g, first + g - 1, None) for g in range(1, n_groups)]
    carries, accs = sweep(jobs, (jnp.zeros((tile, 1), F32),) * n_groups)

    def older(t, g):
        return first + g - 2 - t

    def cond(state):
        t, carries, _ = state
        more = False
        for g in range(n_groups):
            live = jnp.logical_and(older(t, g) >= 0, jnp.min(carries[g]) < SB_UNDERFLOW)
            more = jnp.logical_or(more, live)
        return more

    def body(state):
        t, carries, accs = state
        jobs = [(g, jnp.maximum(older(t, g), 0), exists(older(t, g))) for g in range(n_groups)]
        carries, outs = sweep(jobs, carries)
        return t + 1, carries, tuple(a + o for a, o in zip(accs, outs))

    _, _, accs = lax.while_loop(cond, body, (0, carries, accs))
    for g in range(n_groups):
        gate = g_ref[g * tile:(g + 1) * tile, :].astype(F32)
        o_ref[g * tile:(g + 1) * tile, :] = (accs[g] * (gate * jax.nn.sigmoid(gate))).astype(BF16)


def _sb_attention(proj, batch, seq):
    tile = SB_TILE
    rows = SB_QTILES * tile
    nq = seq // rows
    h = SB_HEADS
    kern = functools.partial(_sb_kernel, tile=tile)
    return pl.pallas_call(
        kern,
        grid=(batch, h, nq),
        in_specs=[
            pl.BlockSpec((rows, HEAD_DIM), lambda b, hh, i: (b * nq + i, hh)),
            pl.BlockSpec((seq, HEAD_DIM), lambda b, hh, i: (b, h + hh)),
            pl.BlockSpec((seq, HEAD_DIM), lambda b, hh, i: (b, 2 * h + hh)),
            pl.BlockSpec((rows, HEAD_DIM), lambda b, hh, i: (b * nq + i, 3 * h + hh)),
        ],
        out_specs=pl.BlockSpec((rows, HEAD_DIM), lambda b, hh, i: (b * nq + i, hh)),
        out_shape=jax.ShapeDtypeStruct((batch * seq, h * HEAD_DIM), BF16),
        compiler_params=_cparams(3),
        name="sb_attention",
    )(proj, proj, proj, proj)


def _s5_local_kernel(u_ref, mre_ref, mim_ref, sre_ref, sim_ref):
    parts_re, parts_im = [], []
    for g in range(2):
        u = u_ref[g]
        parts_re.append(jnp.dot(u, mre_ref[g], preferred_element_type=F32))
        parts_im.append(jnp.dot(u, mim_ref[g], preferred_element_type=F32))
    sre_ref[...] = jnp.concatenate(parts_re, axis=1)
    sim_ref[...] = jnp.concatenate(parts_im, axis=1)


def _s5_local_state(u_flat, m_in_re, m_in_im):
    groups, rows, flat = u_flat.shape
    p = m_in_re.shape[2]
    return pl.pallas_call(
        _s5_local_kernel,
        grid=(groups // 2,),
        in_specs=[
            pl.BlockSpec((2, rows, flat), lambda g: (g, 0, 0)),
            pl.BlockSpec((2, flat, p), lambda g: (g, 0, 0)),
            pl.BlockSpec((2, flat, p), lambda g: (g, 0, 0)),
        ],
        out_specs=[
            pl.BlockSpec((rows, 2 * p), lambda g: (0, g)),
            pl.BlockSpec((rows, 2 * p), lambda g: (0, g)),
        ],
        out_shape=[jax.ShapeDtypeStruct((rows, groups * p), F32)] * 2,
        compiler_params=_cparams(1),
        name="s5_local_state",
    )(u_flat, m_in_re, m_in_im)


def _s5_scan_kernel(sre_ref, sim_ref, are_ref, aim_ref, ore_ref, oim_ref, *, batch):
    rows, width = sre_ref.shape
    per_tile = SUBLANES // batch
    a_re = are_ref[...]
    a_im = aim_ref[...]
    slot = lax.broadcasted_iota(jnp.int32, (SUBLANES, width), 0) // batch

    def spread(x, j):
        if per_tile == 1:
            return x
        return jnp.where(slot == j, x, pltpu.roll(x, batch, 0))

    def body(k, state):
        st_re, st_im = state
        tile_rows = pl.ds(pl.multiple_of(k * SUBLANES, SUBLANES), SUBLANES)
        x_re = sre_ref[tile_rows, :]
        x_im = sim_ref[tile_rows, :]
        out_re, out_im = st_re, st_im
        for j in range(per_tile):
            new_re = a_re * st_re - a_im * st_im + x_re
            new_im = a_re * st_im + a_im * st_re + x_im
            st_re, st_im = spread(new_re, j), spread(new_im, j)
            if j + 1 < per_tile:
                out_re = jnp.where(slot == j + 1, st_re, out_re)
                out_im = jnp.where(slot == j + 1, st_im, out_im)
        ore_ref[tile_rows, :] = out_re
        oim_ref[tile_rows, :] = out_im
        return st_re, st_im

    zero = jnp.zeros((SUBLANES, width), F32)
    lax.fori_loop(0, rows // SUBLANES, body, (zero, zero))


def _s5_scan(s_re, s_im, a_re, a_im, batch):
    rows, width = s_re.shape
    assert batch in (SUBLANES // 2, SUBLANES) and rows % SUBLANES == 0
    wt = 128
    blk = pl.BlockSpec((rows, wt), lambda j: (0, j))
    vec = pl.BlockSpec((1, wt), lambda j: (0, j))
    return pl.pallas_call(
        functools.partial(_s5_scan_kernel, batch=batch),
        grid=(width // wt,),
        in_specs=[blk, blk, vec, vec],
        out_specs=[blk, blk],
        out_shape=[jax.ShapeDtypeStruct((rows, width), F32)] * 2,
        compiler_params=_cparams(1),
        name="s5_scan",
    )(s_re, s_im, a_re, a_im)


def _s5_output_kernel(u_ref, t_ref, pre_ref, pim_ref, ore_ref, oim_ref, y_ref):
    flat = t_ref.shape[1]
    carried = (jnp.dot(pre_ref[...].astype(BF16), ore_ref[0], preferred_element_type=F32)
               + jnp.dot(pim_ref[...].astype(BF16), oim_ref[0], preferred_element_type=F32))
    for g in range(2):
        cols = slice(g * flat, (g + 1) * flat)
        local = jnp.dot(u_ref[g], t_ref[g], preferred_element_type=F32)
        y_ref[0, :, cols] = (local + carried[:, cols]).astype(BF16)


def _s5_output(u_flat, toeplitz, prev_re, prev_im, m_out_re, m_out_im):
    groups, rows, flat = u_flat.shape
    pp = m_out_re.shape[1]
    return pl.pallas_call(
        _s5_output_kernel,
        grid=(groups // 2,),
        in_specs=[
            pl.BlockSpec((2, rows, flat), lambda g: (g, 0, 0)),
            pl.BlockSpec((2, flat, flat), lambda g: (g, 0, 0)),
            pl.BlockSpec((rows, pp), lambda g: (0, g)),
            pl.BlockSpec((rows, pp), lambda g: (0, g)),
            pl.BlockSpec((1, pp, 2 * flat), lambda g: (g, 0, 0)),
            pl.BlockSpec((1, pp, 2 * flat), lambda g: (g, 0, 0)),
        ],
        out_specs=pl.BlockSpec((1, rows, 2 * flat), lambda g: (g, 0, 0)),
        out_shape=jax.ShapeDtypeStruct((groups // 2, rows, 2 * flat), BF16),
        compiler_params=_cparams(1),
        name="s5_output",
    )(u_flat, toeplitz, prev_re, prev_im, m_out_re, m_out_im)


def _s5_epilogue_kernel(y_ref, u_ref, g_ref, d_ref, w_ref, o_ref):
    y = y_ref[...].astype(F32) + d_ref[...] * u_ref[...].astype(F32)
    y = jax.nn.gelu(y)
    gate = jnp.dot(y.astype(BF16), w_ref[...], preferred_element_type=F32)
    y = y * jax.nn.sigmoid(gate)
    g = g_ref[...].astype(F32)
    o_ref[...] = (y * (g * jax.nn.sigmoid(g))).astype(BF16)


def _s5_epilogue(y, proj, d_skip, w_glu_bf16, u_col, g_col):
    t, w = y.shape
    tm = 1024
    return pl.pallas_call(
        _s5_epilogue_kernel,
        grid=(t // tm,),
        in_specs=[
            pl.BlockSpec((tm, w), lambda i: (i, 0)),
            pl.BlockSpec((tm, w), lambda i: (i, u_col)),
            pl.BlockSpec((tm, w), lambda i: (i, g_col)),
            pl.BlockSpec((1, w), lambda i: (0, 0)),
            pl.BlockSpec((w, w), lambda i: (0, 0)),
        ],
        out_specs=pl.BlockSpec((tm, w), lambda i: (i, 0)),
        out_shape=jax.ShapeDtypeStruct((t, w), BF16),
        compiler_params=_cparams(1),
        name="s5_epilogue",
    )(y, proj, proj, d_skip.reshape(1, w), w_glu_bf16)


def _s5_operators(a_re, a_im, log_dt, b_re, b_im, c_re, c_im):
    hi = lax.Precision.HIGHEST
    length = SSM_CHUNK
    lam_re, lam_im = a_re.astype(F32), a_im.astype(F32)
    dt = jnp.exp(log_dt.astype(F32))[:, None]
    steps = jnp.arange(length + 1, dtype=F32)[None, :, None]
    mag = jnp.exp((lam_re * dt)[:, None, :] * steps)
    ang = (lam_im * dt)[:, None, :] * steps
    pw_re, pw_im = mag * jnp.cos(ang), mag * jnp.sin(ang)
    num_re, num_im = pw_re[:, 1] - 1.0, pw_im[:, 1]
    den = lam_re * lam_re + lam_im * lam_im
    quo_re = (num_re * lam_re + num_im * lam_im) / den
    quo_im = (num_im * lam_re - num_re * lam_im) / den
    b_re, b_im = b_re.astype(F32), b_im.astype(F32)
    bb_re = quo_re[..., None] * b_re - quo_im[..., None] * b_im
    bb_im = quo_re[..., None] * b_im + quo_im[..., None] * b_re
    c_re, c_im = c_re.astype(F32)[:, None], c_im.astype(F32)[:, None]
    cp_re = c_re * pw_re[:, :, None, :] - c_im * pw_im[:, :, None, :]
    cp_im = c_re * pw_im[:, :, None, :] + c_im * pw_re[:, :, None, :]

    lag_kernel = (jnp.einsum('gnip,gpj->gjni', cp_re[:, :length], bb_re, precision=hi)
                  - jnp.einsum('gnip,gpj->gjni', cp_im[:, :length], bb_im, precision=hi)).astype(BF16)
    period = jnp.concatenate([lag_kernel, jnp.zeros_like(lag_kernel), jnp.zeros_like(lag_kernel[:, :, :1])],
                             axis=2)
    skew = jnp.tile(period, (1, 1, length, 1))[:, :, :2 * length * length]
    skew = skew.reshape(SSM_GROUPS, SSM_GROUP, length, 2 * length, SSM_GROUP)[:, :, :, :length]
    toeplitz = skew.transpose(0, 2, 1, 3, 4).reshape(SSM_GROUPS, SSM_FLAT, SSM_FLAT)

    rev = length - 1 - jnp.arange(length)
    in_re = pw_re[:, rev, :, None] * bb_re[:, None] - pw_im[:, rev, :, None] * bb_im[:, None]
    in_im = pw_re[:, rev, :, None] * bb_im[:, None] + pw_im[:, rev, :, None] * bb_re[:, None]

    def flat_in(m):
        return m.transpose(0, 1, 3, 2).reshape(SSM_GROUPS, SSM_FLAT, SSM_STATE).astype(BF16)

    def flat_out(m):
        m = m.transpose(0, 3, 1, 2).reshape(SSM_GROUPS // 2, 2, SSM_STATE, SSM_FLAT)
        z = jnp.zeros_like(m[:, 0])
        top = jnp.concatenate([m[:, 0], z], axis=2)
        bot = jnp.concatenate([z, m[:, 1]], axis=2)
        return jnp.concatenate([top, bot], axis=1).astype(BF16)

    return dict(
        toeplitz=toeplitz,
        m_in_re=flat_in(in_re), m_in_im=flat_in(in_im),
        m_out_re=flat_out(cp_re[:, 1:]), m_out_im=flat_out(-cp_im[:, 1:]),
        a_re=pw_re[:, length].reshape(1, -1), a_im=pw_im[:, length].reshape(1, -1),
    )


def _s5_ssm(proj, batch, seq, ops, d_skip, w_glu_bf16, u_col0):
    t = batch * seq
    n_chunks = seq // SSM_CHUNK
    n_rows = batch * n_chunks
    u = proj[:, u_col0:u_col0 + SSM_WIDTH]
    u_flat = (u.reshape(batch, n_chunks, SSM_CHUNK, SSM_GROUPS, SSM_GROUP)
              .transpose(3, 1, 0, 2, 4).reshape(SSM_GROUPS, n_rows, SSM_FLAT))
    s_re, s_im = _s5_local_state(u_flat, ops['m_in_re'], ops['m_in_im'])
    prev_re, prev_im = _s5_scan(s_re, s_im, ops['a_re'], ops['a_im'], batch)
    y_flat = _s5_output(u_flat, ops['toeplitz'], prev_re, prev_im, ops['m_out_re'], ops['m_out_im'])
    y = (y_flat.reshape(SSM_GROUPS // 2, n_chunks, batch, 2, SSM_CHUNK, SSM_GROUP)
         .transpose(2, 1, 4, 0, 3, 5).reshape(t, SSM_WIDTH))
    return _s5_epilogue(y, proj, d_skip, w_glu_bf16, u_col0 // SSM_WIDTH, u_col0 // SSM_WIDTH + 1)


def _moba_kernel(q_ref, k_ref, v_ref, g_ref, o_ref, kaug_ref, kmean_ref, *, seq, q_scale):
    blk = MOBA_BLOCK
    nb = seq // blk
    i = pl.program_id(2)

    @pl.when(i == 0)
    def _():
        kmean_ref[...] = jnp.zeros_like(kmean_ref)
        block_lane = lax.broadcasted_iota(jnp.int32, (blk, HEAD_DIM), 1)

        def per_block(j, _):
            ks = pl.multiple_of(j * blk, blk)
            rows = k_ref[pl.ds(ks, blk), :]
            kaug_ref[pl.ds(ks, blk), :HEAD_DIM] = rows
            kaug_ref[pl.ds(ks, blk), HEAD_DIM:] = (block_lane == j).astype(BF16)
            kmean_ref[pl.ds(j, 1), :] = jnp.sum(rows.astype(F32), axis=0, keepdims=True) * (1.0 / blk)
            return 0

        lax.fori_loop(0, nb, per_block, 0)

    tq = MOBA_QBLOCKS * blk
    first_qb = i * MOBA_QBLOCKS
    q = q_ref[...]
    kmean = kmean_ref[...]
    km_hi = kmean.astype(BF16)
    km_lo = (kmean - km_hi.astype(F32)).astype(BF16)
    contract_last = (((1,), (1,)), ((), ()))

    gate_t = (lax.dot_general(km_hi, q, contract_last, preferred_element_type=F32)
              + lax.dot_general(km_lo, q, contract_last, preferred_element_type=F32))[:nb]
    key_blk = lax.broadcasted_iota(jnp.int32, (nb, tq), 0)
    q_col = lax.broadcasted_iota(jnp.int32, (nb, tq), 1)
    q_block_t = first_qb
    for extra in range(1, MOBA_QBLOCKS):
        q_block_t = q_block_t + (q_col >= extra * blk).astype(jnp.int32)
    key_blk_f = key_blk.astype(F32)
    valid = jnp.logical_and(key_blk < q_block_t, gate_t > 0.5 * NEG * q_scale)
    live = jnp.where(valid, gate_t, -jnp.inf)
    selected = jnp.zeros((nb, tq), jnp.bool_)
    for _ in range(MOBA_TOPK):
        top = jnp.max(live, axis=0, keepdims=True)
        hit = jnp.logical_and(live == top, top > -jnp.inf)
        first = jnp.min(jnp.where(hit, key_blk_f, 1e9), axis=0, keepdims=True)
        pick = key_blk_f == first
        selected = jnp.logical_or(selected, pick)
        live = jnp.where(pick, -jnp.inf, live)
    sel_t = jnp.concatenate([selected.astype(F32), jnp.zeros((HEAD_DIM - nb, tq), F32)], axis=0)
    sel = sel_t.T > 0.5
    lane = lax.broadcasted_iota(jnp.int32, (tq, HEAD_DIM), 1)
    row = lax.broadcasted_iota(jnp.int32, (tq, HEAD_DIM), 0)
    q_block = first_qb
    for extra in range(1, MOBA_QBLOCKS):
        q_block = q_block + (row >= extra * blk).astype(jnp.int32)
    bias_past = jnp.where(jnp.logical_and(sel, lane < first_qb), 0.0, NEG).astype(BF16)
    bias_own = jnp.where(jnp.logical_or(sel, lane == q_block), 0.0, NEG).astype(BF16)
    q_past = jnp.concatenate([q, bias_past], axis=1)
    q_own = jnp.concatenate([q, bias_own], axis=1)

    rows_per = MOBA_ROWS
    n_groups = tq // rows_per
    own = pl.multiple_of(first_qb * blk, blk)
    scores = []
    for r in range(n_groups):
        n_own = ((r * rows_per) // blk + 1) * blk
        s = lax.dot_general(q_own[r * rows_per:(r + 1) * rows_per], kaug_ref[pl.ds(own, n_own), :],
                            contract_last, preferred_element_type=F32)
        q_row = lax.broadcasted_iota(jnp.int32, (rows_per, n_own), 0) + r * rows_per
        k_col = lax.broadcasted_iota(jnp.int32, (rows_per, n_own), 1)
        scores.append(jnp.where(k_col <= q_row, s, -jnp.inf))
    soft = []
    for s in scores:
        m = jnp.max(s, axis=1, keepdims=True)
        p = jnp.exp2(s - m)
        soft.append((m, jnp.sum(p, axis=1, keepdims=True), p.astype(BF16)))
    state = []
    for m, l, p in soft:
        acc = jnp.dot(p, v_ref[pl.ds(own, p.shape[1]), :], preferred_element_type=F32)
        state.append((m, l, acc))

    chunk = MOBA_KCHUNK * blk

    def body(c, state):
        ks = pl.multiple_of(c * chunk, chunk)
        k_chunk = kaug_ref[pl.ds(ks, chunk), :]
        v_chunk = v_ref[pl.ds(ks, chunk), :]
        scores = [lax.dot_general(q_past[r * rows_per:(r + 1) * rows_per], k_chunk, contract_last,
                                  preferred_element_type=F32) for r in range(n_groups)]
        soft = []
        for (m, l, acc), s in zip(state, scores):
            m_new = jnp.maximum(m, jnp.max(s, axis=1, keepdims=True))
            alpha = jnp.exp2(m - m_new)
            p = jnp.exp2(s - m_new)
            soft.append((m_new, alpha * l + jnp.sum(p, axis=1, keepdims=True), alpha * acc,
                         p.astype(BF16)))
        return tuple((m, l, acc + jnp.dot(p, v_chunk, preferred_element_type=F32))
                     for m, l, acc, p in soft)

    n_chunks = (first_qb + MOBA_KCHUNK - 1) // MOBA_KCHUNK
    state = lax.fori_loop(0, n_chunks, body, tuple(state))
    for r in range(tq // rows_per):
        _, l, acc = state[r]
        rows = slice(r * rows_per, (r + 1) * rows_per)
        g = g_ref[rows, :].astype(F32)
        o_ref[rows, :] = (acc / l * (g * jax.nn.sigmoid(g))).astype(BF16)


def _moba_attention(proj, batch, seq, q_scale):
    blk = MOBA_QBLOCKS * MOBA_BLOCK
    n_blocks = seq // MOBA_BLOCK
    assert seq % (MOBA_KCHUNK * MOBA_BLOCK) == 0 and n_blocks % 8 == 0 and n_blocks <= HEAD_DIM
    nq = seq // blk
    h = MOBA_HEADS
    kern = functools.partial(_moba_kernel, seq=seq, q_scale=q_scale)
    return pl.pallas_call(
        kern,
        grid=(batch, h, nq),
        in_specs=[
            pl.BlockSpec((blk, HEAD_DIM), lambda b, hh, i: (b * nq + i, hh)),
            pl.BlockSpec((seq, HEAD_DIM), lambda b, hh, i: (b, h + hh)),
            pl.BlockSpec((seq, HEAD_DIM), lambda b, hh, i: (b, 2 * h + hh)),
            pl.BlockSpec((blk, HEAD_DIM), lambda b, hh, i: (b * nq + i, 3 * h + hh)),
        ],
        out_specs=pl.BlockSpec((blk, HEAD_DIM), lambda b, hh, i: (b * nq + i, hh)),
        out_shape=jax.ShapeDtypeStruct((batch * seq, h * HEAD_DIM), BF16),
        scratch_shapes=[pltpu.VMEM((seq, 2 * HEAD_DIM), BF16), pltpu.VMEM((HEAD_DIM, HEAD_DIM), F32)],
        compiler_params=_cparams(3),
        name="moba_attention",
    )(proj, proj, proj, proj)


def _out_proj_kernel(*refs, n_in, alpha):
    a_refs = refs[:n_in]
    w_refs = refs[n_in:2 * n_in]
    x_ref, gate_ref, lng_ref, lnb_ref, o_ref = refs[2 * n_in:]
    y = jnp.dot(a_refs[0][...], w_refs[0][...], preferred_element_type=F32)
    for a_ref, w_ref in zip(a_refs[1:], w_refs[1:]):
        y = y + jnp.dot(a_ref[...], w_ref[...], preferred_element_type=F32)
    z = alpha * x_ref[...] + (1.0 + gate_ref[0]) * y
    mu = jnp.mean(z, axis=-1, keepdims=True)
    zc = z - mu
    var = jnp.mean(zc * zc, axis=-1, keepdims=True)
    o_ref[...] = zc * lax.rsqrt(var + LN_EPS) * lng_ref[...] + lnb_ref[...]


def _out_proj_ln(acts, weights, x2, gate, ln_g, ln_b, seq, alpha):
    t, d = x2.shape
    tm = 512
    per_batch = seq // tm
    n_in = len(acts)
    kern = functools.partial(_out_proj_kernel, n_in=n_in, alpha=alpha)
    in_specs = ([pl.BlockSpec((tm, a.shape[1]), lambda i: (i, 0)) for a in acts]
                + [pl.BlockSpec(w.shape, lambda i: (0, 0)) for w in weights]
                + [pl.BlockSpec((tm, d), lambda i: (i, 0)),
                   pl.BlockSpec((1, 1, d), lambda i: (i // per_batch, 0, 0)),
                   pl.BlockSpec((1, d), lambda i: (0, 0)),
                   pl.BlockSpec((1, d), lambda i: (0, 0))])
    return pl.pallas_call(
        kern,
        grid=(t // tm,),
        in_specs=in_specs,
        out_specs=pl.BlockSpec((tm, d), lambda i: (i, 0)),
        out_shape=jax.ShapeDtypeStruct((t, d), F32),
        compiler_params=_cparams(1),
        name="out_proj_ln",
    )(*acts, *weights, x2, gate, ln_g.reshape(1, d), ln_b.reshape(1, d))


def kernel(x, c, ada_w, ada_b, ln_g, ln_b, even_w_in, even_w_out, ssm_a_re, ssm_a_im, ssm_log_dt,
           ssm_b_re, ssm_b_im, ssm_c_re, ssm_c_im, ssm_d, ssm_w_glu, odd_w_in, odd_w_out):
    batch, seq, d = x.shape
    depth = ada_w.shape[0]
    alpha = (2 * depth) ** 0.25
    q_scale = HEAD_DIM ** -0.5
    sb_width = SB_HEADS * HEAD_DIM

    c_pad = jnp.pad(c, ((0, 8 - batch), (0, 0)))
    mods = _ada_mod(c_pad, ada_w, ada_b)[:, :batch]
    x2 = x.reshape(batch * seq, d)
    for layer in range(depth):
        shift = mods[layer, :, None, :d]
        scale = mods[layer, :, None, d:2 * d]
        gate = mods[layer, :, None, 2 * d:]
        idx = layer // 2
        if layer % 2 == 0:
            proj = _in_proj(x2, scale, shift, even_w_in[idx].astype(BF16), seq,
                            q_width=sb_width, q_scale=q_scale)
            o_sb = _sb_attention(proj, batch, seq)
            ops = _s5_operators(ssm_a_re[idx], ssm_a_im[idx], ssm_log_dt[idx], ssm_b_re[idx],
                                ssm_b_im[idx], ssm_c_re[idx], ssm_c_im[idx])
            o_ssm = _s5_ssm(proj, batch, seq, ops, ssm_d[idx], ssm_w_glu[idx].astype(BF16),
                            4 * sb_width)
            w_out = even_w_out[idx].astype(BF16)
            acts, weights = [o_sb, o_ssm], [w_out[:sb_width], w_out[sb_width:]]
        else:
            q_scale2 = q_scale * math.log2(math.e)
            proj = _in_proj(x2, scale, shift, odd_w_in[idx].astype(BF16), seq,
                            q_width=MOBA_HEADS * HEAD_DIM, q_scale=q_scale2)
            o = _moba_attention(proj, batch, seq, q_scale2)
            acts, weights = [o], [odd_w_out[idx].astype(BF16)]
        x2 = _out_proj_ln(acts, weights, x2, gate, ln_g[layer], ln_b[layer], seq, alpha)
    return x2.reshape(batch, seq, d)
```

```python
import functools
import math

import jax
import jax.numpy as jnp
from jax import lax
from jax.experimental import pallas as pl
from jax.experimental.pallas import tpu as pltpu

F32 = jnp.float32
BF16 = jnp.bfloat16

HEAD_DIM = 128
SB_HEADS = 12
SSM_WIDTH = 512
SSM_GROUP = 16
SSM_GROUPS = 32
SSM_STATE = 64
MOBA_HEADS = 16
MOBA_BLOCK = 256
MOBA_TOPK = 3
LN_EPS = 1e-5
NEG = -1e30

LANES = 128
SSM_CHUNK = 8
SSM_TILES = SSM_WIDTH // LANES
SSM_TILE_GROUPS = LANES // SSM_GROUP
SSM_FLAT = SSM_CHUNK * LANES
SB_TILE = 256
SB_QTILES = 4
MOBA_QBLOCKS = 4
MOBA_KCHUNK = 4
MOBA_ROWS = 256
SB_UNDERFLOW = 105.0

VMEM_LIMIT = 56 * 1024 * 1024
SUBLANES = 8


def _cparams(n_axes):
    return pltpu.CompilerParams(
        dimension_semantics=("arbitrary",) * n_axes, vmem_limit_bytes=VMEM_LIMIT)


def _ada_kernel(c_ref, w_ref, b_ref, o_ref):
    c = c_ref[...]
    cond = c * jax.nn.sigmoid(c)
    o_ref[0] = jnp.dot(cond, w_ref[0], preferred_element_type=F32,
                       precision=lax.Precision.HIGHEST) + b_ref[0]


def _ada_mod(c_pad, ada_w, ada_b):
    depth, d, n = ada_w.shape
    rows = c_pad.shape[0]
    tn = 1024
    return pl.pallas_call(
        _ada_kernel,
        grid=(depth, n // tn),
        in_specs=[
            pl.BlockSpec((rows, d), lambda l, j: (0, 0)),
            pl.BlockSpec((1, d, tn), lambda l, j: (l, 0, j)),
            pl.BlockSpec((1, 1, tn), lambda l, j: (l, 0, j)),
        ],
        out_specs=pl.BlockSpec((1, rows, tn), lambda l, j: (l, 0, j)),
        out_shape=jax.ShapeDtypeStruct((depth, rows, n), F32),
        compiler_params=_cparams(2),
        name="ada_mod",
    )(c_pad, ada_w, ada_b.reshape(depth, 1, n))


def _in_proj_kernel(x_ref, sc_ref, sh_ref, w_ref, o_ref, h_ref, *, q_width, q_scale):
    j = pl.program_id(1)
    tn = o_ref.shape[1]

    @pl.when(j == 0)
    def _():
        h = x_ref[...] * (1.0 + sc_ref[0]) + sh_ref[0]
        h_ref[...] = h.astype(BF16)

    acc = jnp.dot(h_ref[...], w_ref[...], preferred_element_type=F32)
    column = j * tn + lax.broadcasted_iota(jnp.int32, (1, tn), 1)
    o_ref[...] = (acc * jnp.where(column < q_width, q_scale, 1.0)).astype(BF16)


def _in_proj(x2, scale, shift, w_bf16, seq, *, q_width, q_scale):
    t, d = x2.shape
    n = w_bf16.shape[1]
    tm, tn = 1024, 1024
    per_batch = seq // tm
    kern = functools.partial(_in_proj_kernel, q_width=q_width, q_scale=q_scale)
    return pl.pallas_call(
        kern,
        grid=(t // tm, n // tn),
        in_specs=[
            pl.BlockSpec((tm, d), lambda i, j: (i, 0)),
            pl.BlockSpec((1, 1, d), lambda i, j: (i // per_batch, 0, 0)),
            pl.BlockSpec((1, 1, d), lambda i, j: (i // per_batch, 0, 0)),
            pl.BlockSpec((d, tn), lambda i, j: (0, j)),
        ],
        out_specs=pl.BlockSpec((tm, tn), lambda i, j: (i, j)),
        out_shape=jax.ShapeDtypeStruct((t, n), BF16),
        scratch_shapes=[pltpu.VMEM((tm, d), BF16)],
        compiler_params=_cparams(2),
        name="in_proj",
    )(x2, scale, shift, w_bf16)


def _sb_kernel(q_ref, k_ref, v_ref, g_ref, o_ref, *, tile):
    i = pl.program_id(2)
    n_groups = SB_QTILES
    qs = [q_ref[g * tile:(g + 1) * tile, :] for g in range(n_groups)]
    row = lax.broadcasted_iota(jnp.int32, (tile, tile), 0)
    col = lax.broadcasted_iota(jnp.int32, (tile, tile), 1)
    later_key = (row > col).astype(BF16)

    def sweep(jobs, carries):
        starts = [pl.multiple_of(blk * tile, tile) for _, blk, _ in jobs]
        zs = [lax.dot_general(qs[g], k_ref[pl.ds(ks, tile), :], (((1,), (1,)), ((), ())),
                              preferred_element_type=F32) for (g, _, _), ks in zip(jobs, starts)]
        softplus = []
        for z, (_, _, past) in zip(zs, jobs):
            sp = jnp.maximum(z, 0.0) + jnp.log(1.0 + jnp.exp(-jnp.abs(z)))
            softplus.append(sp if past is None else jnp.where(past, sp, 0.0))
        within = [jnp.dot(sp.astype(BF16), later_key, preferred_element_type=F32) for sp in softplus]
        carries = list(carries)
        weights = []
        for z, sp, inner, (g, _, past) in zip(zs, softplus, within, jobs):
            w = jnp.exp(z - sp - inner - carries[g])
            weights.append((w if past is None else jnp.where(past, w, 0.0)).astype(BF16))
            carries[g] = carries[g] + jnp.sum(sp, axis=1, keepdims=True)
        outs = [None] * n_groups
        for w, ks, (g, _, _) in zip(weights, starts, jobs):
            part = jnp.dot(w, v_ref[pl.ds(ks, tile), :], preferred_element_type=F32)
            outs[g] = part if outs[g] is None else outs[g] + part
        return tuple(carries), tuple(outs)

    def exists(blk):
        return jnp.broadcast_to(blk >= 0, (tile, tile))

    first = i * n_groups
    jobs = [(g, first + g, col < row) for g in range(n_groups)]
    jobs.append((0, jnp.maximum(first - 1, 0), exists(first - 1)))
    jobs += [(g, first + g - 1, None) for g in range(1, n_groups)]
    carries, accs = sweep(jobs, (jnp.zeros((tile, 1), F32),) * n_groups)

    def older(t, g):
        return first + g - 2 - t

    def cond(state):
        t, carries, _ = state
        more = False
        for g in range(n_groups):
            live = jnp.logical_and(older(t, g) >= 0, jnp.min(carries[g]) < SB_UNDERFLOW)
            more = jnp.logical_or(more, live)
        return more

    def body(state):
        t, carries, accs = state
        jobs = [(g, jnp.maximum(older(t, g), 0), exists(older(t, g))) for g in range(n_groups)]
        carries, outs = sweep(jobs, carries)
        return t + 1, carries, tuple(a + o for a, o in zip(accs, outs))

    _, _, accs = lax.while_loop(cond, body, (0, carries, accs))
    for g in range(n_groups):
        gate = g_ref[g * tile:(g + 1) * tile, :].astype(F32)
        o_ref[g * tile:(g + 1) * tile, :] = (accs[g] * (gate * jax.nn.sigmoid(gate))).astype(BF16)


def _sb_attention(proj, batch, seq):
    tile = SB_TILE
    rows = SB_QTILES * tile
    nq = seq // rows
    h = SB_HEADS
    kern = functools.partial(_sb_kernel, tile=tile)
    return pl.pallas_call(
        kern,
        grid=(batch, h, nq),
        in_specs=[
            pl.BlockSpec((rows, HEAD_DIM), lambda b, hh, i: (b * nq + i, hh)),
            pl.BlockSpec((seq, HEAD_DIM), lambda b, hh, i: (b, h + hh)),
            pl.BlockSpec((seq, HEAD_DIM), lambda b, hh, i: (b, 2 * h + hh)),
            pl.BlockSpec((rows, HEAD_DIM), lambda b, hh, i: (b * nq + i, 3 * h + hh)),
        ],
        out_specs=pl.BlockSpec((rows, HEAD_DIM), lambda b, hh, i: (b * nq + i, hh)),
        out_shape=jax.ShapeDtypeStruct((batch * seq, h * HEAD_DIM), BF16),
        compiler_params=_cparams(3),
        name="sb_attention",
    )(proj, proj, proj, proj)


def _s5_local_kernel(u_ref, mre_ref, mim_ref, sre_ref, sim_ref):
    u = u_ref[0]
    sre_ref[...] = jnp.dot(u, mre_ref[0], preferred_element_type=F32)
    sim_ref[...] = jnp.dot(u, mim_ref[0], preferred_element_type=F32)


def _s5_row_block(rows):
    return min(rows, 1024)


def _s5_local_state(u_flat, m_in_re, m_in_im):
    tiles, rows, flat = u_flat.shape
    p = m_in_re.shape[2]
    rb = _s5_row_block(rows)
    return pl.pallas_call(
        _s5_local_kernel,
        grid=(tiles, rows // rb),
        in_specs=[
            pl.BlockSpec((1, rb, flat), lambda g, r: (g, r, 0)),
            pl.BlockSpec((1, flat, p), lambda g, r: (g, 0, 0)),
            pl.BlockSpec((1, flat, p), lambda g, r: (g, 0, 0)),
        ],
        out_specs=[
            pl.BlockSpec((rb, p), lambda g, r: (r, g)),
            pl.BlockSpec((rb, p), lambda g, r: (r, g)),
        ],
        out_shape=[jax.ShapeDtypeStruct((rows, tiles * p), F32)] * 2,
        compiler_params=_cparams(2),
        name="s5_local_state",
    )(u_flat, m_in_re, m_in_im)


def _s5_scan_kernel(sre_ref, sim_ref, are_ref, aim_ref, ore_ref, oim_ref, *, batch):
    rows, width = sre_ref.shape
    per_tile = SUBLANES // batch
    a_re = are_ref[...]
    a_im = aim_ref[...]
    slot = lax.broadcasted_iota(jnp.int32, (SUBLANES, width), 0) // batch

    def spread(x, j):
        if per_tile == 1:
            return x
        return jnp.where(slot == j, x, pltpu.roll(x, batch, 0))

    def body(k, state):
        st_re, st_im = state
        tile_rows = pl.ds(pl.multiple_of(k * SUBLANES, SUBLANES), SUBLANES)
        x_re = sre_ref[tile_rows, :]
        x_im = sim_ref[tile_rows, :]
        out_re, out_im = st_re, st_im
        for j in range(per_tile):
            new_re = a_re * st_re - a_im * st_im + x_re
            new_im = a_re * st_im + a_im * st_re + x_im
            st_re, st_im = spread(new_re, j), spread(new_im, j)
            if j + 1 < per_tile:
                out_re = jnp.where(slot == j + 1, st_re, out_re)
                out_im = jnp.where(slot == j + 1, st_im, out_im)
        ore_ref[tile_rows, :] = out_re
        oim_ref[tile_rows, :] = out_im
        return st_re, st_im

    zero = jnp.zeros((SUBLANES, width), F32)
    lax.fori_loop(0, rows // SUBLANES, body, (zero, zero))


def _s5_scan(s_re, s_im, a_re, a_im, batch):
    rows, width = s_re.shape
    assert batch in (SUBLANES // 2, SUBLANES) and rows % SUBLANES == 0
    wt = 128
    blk = pl.BlockSpec((rows, wt), lambda j: (0, j))
    vec = pl.BlockSpec((1, wt), lambda j: (0, j))
    return pl.pallas_call(
        functools.partial(_s5_scan_kernel, batch=batch),
        grid=(width // wt,),
        in_specs=[blk, blk, vec, vec],
        out_specs=[blk, blk],
        out_shape=[jax.ShapeDtypeStruct((rows, width), F32)] * 2,
        compiler_params=_cparams(1),
        name="s5_scan",
    )(s_re, s_im, a_re, a_im)


def _s5_output_kernel(u_ref, t_ref, pre_ref, pim_ref, ore_ref, oim_ref, y_ref):
    y = jnp.dot(u_ref[0], t_ref[0], preferred_element_type=F32)
    y = y + jnp.dot(pre_ref[...].astype(BF16), ore_ref[0], preferred_element_type=F32)
    y = y + jnp.dot(pim_ref[...].astype(BF16), oim_ref[0], preferred_element_type=F32)
    y_ref[0] = y.astype(BF16)


def _s5_output(u_flat, toeplitz, prev_re, prev_im, m_out_re, m_out_im):
    tiles, rows, flat = u_flat.shape
    p = m_out_re.shape[1]
    rb = _s5_row_block(rows)
    return pl.pallas_call(
        _s5_output_kernel,
        grid=(tiles, rows // rb),
        in_specs=[
            pl.BlockSpec((1, rb, flat), lambda g, r: (g, r, 0)),
            pl.BlockSpec((1, flat, flat), lambda g, r: (g, 0, 0)),
            pl.BlockSpec((rb, p), lambda g, r: (r, g)),
            pl.BlockSpec((rb, p), lambda g, r: (r, g)),
            pl.BlockSpec((1, p, flat), lambda g, r: (g, 0, 0)),
            pl.BlockSpec((1, p, flat), lambda g, r: (g, 0, 0)),
        ],
        out_specs=pl.BlockSpec((1, rb, flat), lambda g, r: (g, r, 0)),
        out_shape=jax.ShapeDtypeStruct((tiles, rows, flat), BF16),
        compiler_params=_cparams(2),
        name="s5_output",
    )(u_flat, toeplitz, prev_re, prev_im, m_out_re, m_out_im)


def _s5_epilogue_kernel(y_ref, u_ref, g_ref, d_ref, w_ref, o_ref):
    y = y_ref[...].astype(F32) + d_ref[...] * u_ref[...].astype(F32)
    y = jax.nn.gelu(y)
    gate = jnp.dot(y.astype(BF16), w_ref[...], preferred_element_type=F32)
    y = y * jax.nn.sigmoid(gate)
    g = g_ref[...].astype(F32)
    o_ref[...] = (y * (g * jax.nn.sigmoid(g))).astype(BF16)


def _s5_epilogue(y, proj, d_skip, w_glu_bf16, u_col, g_col):
    t, w = y.shape
    tm = 1024
    return pl.pallas_call(
        _s5_epilogue_kernel,
        grid=(t // tm,),
        in_specs=[
            pl.BlockSpec((tm, w), lambda i: (i, 0)),
            pl.BlockSpec((tm, w), lambda i: (i, u_col)),
            pl.BlockSpec((tm, w), lambda i: (i, g_col)),
            pl.BlockSpec((1, w), lambda i: (0, 0)),
            pl.BlockSpec((w, w), lambda i: (0, 0)),
        ],
        out_specs=pl.BlockSpec((tm, w), lambda i: (i, 0)),
        out_shape=jax.ShapeDtypeStruct((t, w), BF16),
        compiler_params=_cparams(1),
        name="s5_epilogue",
    )(y, proj, proj, d_skip.reshape(1, w), w_glu_bf16)


def _s5_operators(a_re, a_im, log_dt, b_re, b_im, c_re, c_im):
    hi = lax.Precision.HIGHEST
    length = SSM_CHUNK
    lam_re, lam_im = a_re.astype(F32), a_im.astype(F32)
    dt = jnp.exp(log_dt.astype(F32))[:, None]
    steps = jnp.arange(length + 1, dtype=F32)[None, :, None]
    mag = jnp.exp((lam_re * dt)[:, None, :] * steps)
    ang = (lam_im * dt)[:, None, :] * steps
    pw_re, pw_im = mag * jnp.cos(ang), mag * jnp.sin(ang)
    num_re, num_im = pw_re[:, 1] - 1.0, pw_im[:, 1]
    den = lam_re * lam_re + lam_im * lam_im
    quo_re = (num_re * lam_re + num_im * lam_im) / den
    quo_im = (num_im * lam_re - num_re * lam_im) / den
    b_re, b_im = b_re.astype(F32), b_im.astype(F32)
    bb_re = quo_re[..., None] * b_re - quo_im[..., None] * b_im
    bb_im = quo_re[..., None] * b_im + quo_im[..., None] * b_re
    c_re, c_im = c_re.astype(F32)[:, None], c_im.astype(F32)[:, None]
    cp_re = c_re * pw_re[:, :, None, :] - c_im * pw_im[:, :, None, :]
    cp_im = c_re * pw_im[:, :, None, :] + c_im * pw_re[:, :, None, :]

    lag_kernel = (jnp.einsum('gnip,gpj->gjni', cp_re[:, :length], bb_re, precision=hi)
                  - jnp.einsum('gnip,gpj->gjni', cp_im[:, :length], bb_im, precision=hi)).astype(BF16)
    period = jnp.concatenate([lag_kernel, jnp.zeros_like(lag_kernel), jnp.zeros_like(lag_kernel[:, :, :1])],
                             axis=2)
    skew = jnp.tile(period, (1, 1, length, 1))[:, :, :2 * length * length]
    skew = skew.reshape(SSM_GROUPS, SSM_GROUP, length, 2 * length, SSM_GROUP)[:, :, :, :length]

    rev = length - 1 - jnp.arange(length)
    in_re = pw_re[:, rev, :, None] * bb_re[:, None] - pw_im[:, rev, :, None] * bb_im[:, None]
    in_im = pw_re[:, rev, :, None] * bb_im[:, None] + pw_im[:, rev, :, None] * bb_re[:, None]

    tg = SSM_TILE_GROUPS
    same_group = jnp.eye(tg, dtype=F32)

    def per_tile(m, perm, shape):
        m = m.astype(F32).reshape((SSM_TILES, tg) + m.shape[1:])
        m = m[..., None] * same_group.reshape((1, tg) + (1,) * (m.ndim - 2) + (tg,))
        return m.transpose(perm).reshape(shape).astype(BF16)

    return dict(
        toeplitz=per_tile(skew, (0, 3, 1, 2, 4, 6, 5), (SSM_TILES, SSM_FLAT, SSM_FLAT)),
        m_in_re=per_tile(in_re, (0, 2, 1, 4, 5, 3), (SSM_TILES, SSM_FLAT, tg * SSM_STATE)),
        m_in_im=per_tile(in_im, (0, 2, 1, 4, 5, 3), (SSM_TILES, SSM_FLAT, tg * SSM_STATE)),
        m_out_re=per_tile(cp_re[:, 1:], (0, 5, 4, 2, 1, 3), (SSM_TILES, tg * SSM_STATE, SSM_FLAT)),
        m_out_im=per_tile(-cp_im[:, 1:], (0, 5, 4, 2, 1, 3), (SSM_TILES, tg * SSM_STATE, SSM_FLAT)),
        a_re=pw_re[:, length].reshape(1, -1), a_im=pw_im[:, length].reshape(1, -1),
    )


def _s5_ssm(proj, batch, seq, ops, d_skip, w_glu_bf16, u_col0):
    t = batch * seq
    n_chunks = seq // SSM_CHUNK
    n_rows = batch * n_chunks
    lanes = SSM_WIDTH // SSM_TILES
    u = proj[:, u_col0:u_col0 + SSM_WIDTH]
    u_flat = (u.reshape(batch, n_chunks, SSM_CHUNK, SSM_TILES, lanes)
              .transpose(3, 1, 0, 2, 4).reshape(SSM_TILES, n_rows, SSM_FLAT))
    s_re, s_im = _s5_local_state(u_flat, ops['m_in_re'], ops['m_in_im'])
    prev_re, prev_im = _s5_scan(s_re, s_im, ops['a_re'], ops['a_im'], batch)
    y_flat = _s5_output(u_flat, ops['toeplitz'], prev_re, prev_im, ops['m_out_re'], ops['m_out_im'])
    y = (y_flat.reshape(SSM_TILES, n_chunks, batch, SSM_CHUNK, lanes)
         .transpose(2, 1, 3, 0, 4).reshape(t, SSM_WIDTH))
    return _s5_epilogue(y, proj, d_skip, w_glu_bf16, u_col0 // SSM_WIDTH, u_col0 // SSM_WIDTH + 1)


def _moba_kernel(q_ref, k_ref, v_ref, g_ref, o_ref, kaug_ref, kmean_ref, *, seq, q_scale):
    blk = MOBA_BLOCK
    nb = seq // blk
    i = pl.program_id(2)

    @pl.when(i == 0)
    def _():
        kmean_ref[...] = jnp.zeros_like(kmean_ref)
        block_lane = lax.broadcasted_iota(jnp.int32, (blk, HEAD_DIM), 1)

        def per_block(j, _):
            ks = pl.multiple_of(j * blk, blk)
            rows = k_ref[pl.ds(ks, blk), :]
            kaug_ref[pl.ds(ks, blk), :HEAD_DIM] = rows
            kaug_ref[pl.ds(ks, blk), HEAD_DIM:] = (block_lane == j).astype(BF16)
            kmean_ref[pl.ds(j, 1), :] = jnp.sum(rows.astype(F32), axis=0, keepdims=True) * (1.0 / blk)
            return 0

        lax.fori_loop(0, nb, per_block, 0)

    tq = MOBA_QBLOCKS * blk
    first_qb = i * MOBA_QBLOCKS
    q = q_ref[...]
    kmean = kmean_ref[...]
    km_hi = kmean.astype(BF16)
    km_lo = (kmean - km_hi.astype(F32)).astype(BF16)
    contract_last = (((1,), (1,)), ((), ()))

    gate_t = (lax.dot_general(km_hi, q, contract_last, preferred_element_type=F32)
              + lax.dot_general(km_lo, q, contract_last, preferred_element_type=F32))[:nb]
    key_blk = lax.broadcasted_iota(jnp.int32, (nb, tq), 0)
    q_col = lax.broadcasted_iota(jnp.int32, (nb, tq), 1)
    q_block_t = first_qb
    for extra in range(1, MOBA_QBLOCKS):
        q_block_t = q_block_t + (q_col >= extra * blk).astype(jnp.int32)
    key_blk_f = key_blk.astype(F32)
    valid = jnp.logical_and(key_blk < q_block_t, gate_t > 0.5 * NEG * q_scale)
    live = jnp.where(valid, gate_t, -jnp.inf)
    selected = jnp.zeros((nb, tq), jnp.bool_)
    for _ in range(MOBA_TOPK):
        top = jnp.max(live, axis=0, keepdims=True)
        hit = jnp.logical_and(live == top, top > -jnp.inf)
        first = jnp.min(jnp.where(hit, key_blk_f, 1e9), axis=0, keepdims=True)
        pick = key_blk_f == first
        selected = jnp.logical_or(selected, pick)
        live = jnp.where(pick, -jnp.inf, live)
    sel_t = jnp.concatenate([selected.astype(F32), jnp.zeros((HEAD_DIM - nb, tq), F32)], axis=0)
    sel = sel_t.T > 0.5
    lane = lax.broadcasted_iota(jnp.int32, (tq, HEAD_DIM), 1)
    row = lax.broadcasted_iota(jnp.int32, (tq, HEAD_DIM), 0)
    q_block = first_qb
    for extra in range(1, MOBA_QBLOCKS):
        q_block = q_block + (row >= extra * blk).astype(jnp.int32)
    bias_past = jnp.where(jnp.logical_and(sel, lane < first_qb), 0.0, NEG).astype(BF16)
    bias_own = jnp.where(jnp.logical_or(sel, lane == q_block), 0.0, NEG).astype(BF16)
    q_past = jnp.concatenate([q, bias_past], axis=1)
    q_own = jnp.concatenate([q, bias_own], axis=1)

    rows_per = MOBA_ROWS
    n_groups = tq // rows_per
    own = pl.multiple_of(first_qb * blk, blk)
    scores = []
    for r in range(n_groups):
        n_own = ((r * rows_per) // blk + 1) * blk
        s = lax.dot_general(q_own[r * rows_per:(r + 1) * rows_per], kaug_ref[pl.ds(own, n_own), :],
                            contract_last, preferred_element_type=F32)
        q_row = lax.broadcasted_iota(jnp.int32, (rows_per, n_own), 0) + r * rows_per
        k_col = lax.broadcasted_iota(jnp.int32, (rows_per, n_own), 1)
        scores.append(jnp.where(k_col <= q_row, s, -jnp.inf))
    soft = []
    for s in scores:
        m = jnp.max(s, axis=1, keepdims=True)
        p = jnp.exp2(s - m)
        soft.append((m, jnp.sum(p, axis=1, keepdims=True), p.astype(BF16)))
    state = []
    for m, l, p in soft:
        acc = jnp.dot(p, v_ref[pl.ds(own, p.shape[1]), :], preferred_element_type=F32)
        state.append((m, l, acc))

    chunk = MOBA_KCHUNK * blk

    def body(c, state):
        ks = pl.multiple_of(c * chunk, chunk)
        k_chunk = kaug_ref[pl.ds(ks, chunk), :]
        v_chunk = v_ref[pl.ds(ks, chunk), :]
        scores = [lax.dot_general(q_past[r * rows_per:(r + 1) * rows_per], k_chunk, contract_last,
                                  preferred_element_type=F32) for r in range(n_groups)]
        soft = []
        for (m, l, acc), s in zip(state, scores):
            m_new = jnp.maximum(m, jnp.max(s, axis=1, keepdims=True))
            alpha = jnp.exp2(m - m_new)
            p = jnp.exp2(s - m_new)
            soft.append((m_new, alpha * l + jnp.sum(p, axis=1, keepdims=True), alpha * acc,
                         p.astype(BF16)))
        return tuple((m, l, acc + jnp.dot(p, v_chunk, preferred_element_type=F32))
                     for m, l, acc, p in soft)

    n_chunks = (first_qb + MOBA_KCHUNK - 1) // MOBA_KCHUNK
    state = lax.fori_loop(0, n_chunks, body, tuple(state))
    for r in range(tq // rows_per):
        _, l, acc = state[r]
        rows = slice(r * rows_per, (r + 1) * rows_per)
        g = g_ref[rows, :].astype(F32)
        o_ref[rows, :] = (acc / l * (g * jax.nn.sigmoid(g))).astype(BF16)


def _moba_attention(proj, batch, seq, q_scale):
    blk = MOBA_QBLOCKS * MOBA_BLOCK
    n_blocks = seq // MOBA_BLOCK
    assert seq % (MOBA_KCHUNK * MOBA_BLOCK) == 0 and n_blocks % 8 == 0 and n_blocks <= HEAD_DIM
    nq = seq // blk
    h = MOBA_HEADS
    kern = functools.partial(_moba_kernel, seq=seq, q_scale=q_scale)
    return pl.pallas_call(
        kern,
        grid=(batch, h, nq),
        in_specs=[
            pl.BlockSpec((blk, HEAD_DIM), lambda b, hh, i: (b * nq + i, hh)),
            pl.BlockSpec((seq, HEAD_DIM), lambda b, hh, i: (b, h + hh)),
            pl.BlockSpec((seq, HEAD_DIM), lambda b, hh, i: (b, 2 * h + hh)),
            pl.BlockSpec((blk, HEAD_DIM), lambda b, hh, i: (b * nq + i, 3 * h + hh)),
        ],
        out_specs=pl.BlockSpec((blk, HEAD_DIM), lambda b, hh, i: (b * nq + i, hh)),
        out_shape=jax.ShapeDtypeStruct((batch * seq, h * HEAD_DIM), BF16),
        scratch_shapes=[pltpu.VMEM((seq, 2 * HEAD_DIM), BF16), pltpu.VMEM((HEAD_DIM, HEAD_DIM), F32)],
        compiler_params=_cparams(3),
        name="moba_attention",
    )(proj, proj, proj, proj)


def _out_proj_kernel(*refs, n_in, alpha):
    a_refs = refs[:n_in]
    w_refs = refs[n_in:2 * n_in]
    x_ref, gate_ref, lng_ref, lnb_ref, o_ref = refs[2 * n_in:]
    y = jnp.dot(a_refs[0][...], w_refs[0][...], preferred_element_type=F32)
    for a_ref, w_ref in zip(a_refs[1:], w_refs[1:]):
        y = y + jnp.dot(a_ref[...], w_ref[...], preferred_element_type=F32)
    z = alpha * x_ref[...] + (1.0 + gate_ref[0]) * y
    mu = jnp.mean(z, axis=-1, keepdims=True)
    zc = z - mu
    var = jnp.mean(zc * zc, axis=-1, keepdims=True)
    o_ref[...] = zc * lax.rsqrt(var + LN_EPS) * lng_ref[...] + lnb_ref[...]


def _out_proj_ln(acts, weights, x2, gate, ln_g, ln_b, seq, alpha):
    t, d = x2.shape
    tm = 512
    per_batch = seq // tm
    n_in = len(acts)
    kern = functools.partial(_out_proj_kernel, n_in=n_in, alpha=alpha)
    in_specs = ([pl.BlockSpec((tm, a.shape[1]), lambda i: (i, 0)) for a in acts]
                + [pl.BlockSpec(w.shape, lambda i: (0, 0)) for w in weights]
                + [pl.BlockSpec((tm, d), lambda i: (i, 0)),
                   pl.BlockSpec((1, 1, d), lambda i: (i // per_batch, 0, 0)),
                   pl.BlockSpec((1, d), lambda i: (0, 0)),
                   pl.BlockSpec((1, d), lambda i: (0, 0))])
    return pl.pallas_call(
        kern,
        grid=(t // tm,),
        in_specs=in_specs,
        out_specs=pl.BlockSpec((tm, d), lambda i: (i, 0)),
        out_shape=jax.ShapeDtypeStruct((t, d), F32),
        compiler_params=_cparams(1),
        name="out_proj_ln",
    )(*acts, *weights, x2, gate, ln_g.reshape(1, d), ln_b.reshape(1, d))


def kernel(x, c, ada_w, ada_b, ln_g, ln_b, even_w_in, even_w_out, ssm_a_re, ssm_a_im, ssm_log_dt,
           ssm_b_re, ssm_b_im, ssm_c_re, ssm_c_im, ssm_d, ssm_w_glu, odd_w_in, odd_w_out):
    batch, seq, d = x.shape
    depth = ada_w.shape[0]
    alpha = (2 * depth) ** 0.25
    q_scale = HEAD_DIM ** -0.5
    sb_width = SB_HEADS * HEAD_DIM

    c_pad = jnp.pad(c, ((0, 8 - batch), (0, 0)))
    mods = _ada_mod(c_pad, ada_w, ada_b)[:, :batch]
    x2 = x.reshape(batch * seq, d)
    for layer in range(depth):
        shift = mods[layer, :, None, :d]
        scale = mods[layer, :, None, d:2 * d]
        gate = mods[layer, :, None, 2 * d:]
        idx = layer // 2
        if layer % 2 == 0:
            proj = _in_proj(x2, scale, shift, even_w_in[idx].astype(BF16), seq,
                            q_width=sb_width, q_scale=q_scale)
            o_sb = _sb_attention(proj, batch, seq)
            ops = _s5_operators(ssm_a_re[idx], ssm_a_im[idx], ssm_log_dt[idx], ssm_b_re[idx],
                                ssm_b_im[idx], ssm_c_re[idx], ssm_c_im[idx])
            o_ssm = _s5_ssm(proj, batch, seq, ops, ssm_d[idx], ssm_w_glu[idx].astype(BF16),
                            4 * sb_width)
            w_out = even_w_out[idx].astype(BF16)
            acts, weights = [o_sb, o_ssm], [w_out[:sb_width], w_out[sb_width:]]
        else:
            q_scale2 = q_scale * math.log2(math.e)
            proj = _in_proj(x2, scale, shift, odd_w_in[idx].astype(BF16), seq,
                            q_width=MOBA_HEADS * HEAD_DIM, q_scale=q_scale2)
            o = _moba_attention(proj, batch, seq, q_scale2)
            acts, weights = [o], [odd_w_out[idx].astype(BF16)]
        x2 = _out_proj_ln(acts, weights, x2, gate, ln_g[layer], ln_b[layer], seq, alpha)
    return x2.reshape(batch, seq, d)
```

```python
import functools
import math

import jax
import jax.numpy as jnp
from jax import lax
from jax.experimental import pallas as pl
from jax.experimental.pallas import tpu as pltpu

F32 = jnp.float32
BF16 = jnp.bfloat16

HEAD_DIM = 128
SB_HEADS = 12
SSM_WIDTH = 512
SSM_GROUP = 16
SSM_GROUPS = 32
SSM_STATE = 64
MOBA_HEADS = 16
MOBA_BLOCK = 256
MOBA_TOPK = 3
LN_EPS = 1e-5
NEG = -1e30

LANES = 128
SSM_CHUNK = 8
SSM_TILES = SSM_WIDTH // LANES
SSM_TILE_GROUPS = LANES // SSM_GROUP
SSM_FLAT = SSM_CHUNK * LANES
SB_TILE = 256
SB_QTILES = 8
MOBA_QBLOCKS = 8
MOBA_KCHUNK = 4
MOBA_ROWS = 256
SB_UNDERFLOW = 105.0

VMEM_LIMIT = 56 * 1024 * 1024
SUBLANES = 8


def _cparams(n_axes):
    return pltpu.CompilerParams(
        dimension_semantics=("arbitrary",) * n_axes, vmem_limit_bytes=VMEM_LIMIT)


def _ada_kernel(c_ref, w_ref, b_ref, o_ref):
    c = c_ref[...]
    cond = c * jax.nn.sigmoid(c)
    o_ref[0] = jnp.dot(cond, w_ref[0], preferred_element_type=F32,
                       precision=lax.Precision.HIGHEST) + b_ref[0]


def _ada_mod(c_pad, ada_w, ada_b):
    depth, d, n = ada_w.shape
    rows = c_pad.shape[0]
    tn = 1024
    return pl.pallas_call(
        _ada_kernel,
        grid=(depth, n // tn),
        in_specs=[
            pl.BlockSpec((rows, d), lambda l, j: (0, 0)),
            pl.BlockSpec((1, d, tn), lambda l, j: (l, 0, j)),
            pl.BlockSpec((1, 1, tn), lambda l, j: (l, 0, j)),
        ],
        out_specs=pl.BlockSpec((1, rows, tn), lambda l, j: (l, 0, j)),
        out_shape=jax.ShapeDtypeStruct((depth, rows, n), F32),
        compiler_params=_cparams(2),
        name="ada_mod",
    )(c_pad, ada_w, ada_b.reshape(depth, 1, n))


def _in_proj_kernel(x_ref, sc_ref, sh_ref, w_ref, o_ref, h_ref, *, q_width, q_scale):
    j = pl.program_id(1)
    tn = o_ref.shape[1]

    @pl.when(j == 0)
    def _():
        h = x_ref[...] * (1.0 + sc_ref[0]) + sh_ref[0]
        h_ref[...] = h.astype(BF16)

    acc = jnp.dot(h_ref[...], w_ref[...], preferred_element_type=F32)
    column = j * tn + lax.broadcasted_iota(jnp.int32, (1, tn), 1)
    o_ref[...] = (acc * jnp.where(column < q_width, q_scale, 1.0)).astype(BF16)


def _in_proj(x2, scale, shift, w_bf16, seq, *, q_width, q_scale):
    t, d = x2.shape
    n = w_bf16.shape[1]
    tm, tn = 1024, 1024
    per_batch = seq // tm
    kern = functools.partial(_in_proj_kernel, q_width=q_width, q_scale=q_scale)
    return pl.pallas_call(
        kern,
        grid=(t // tm, n // tn),
        in_specs=[
            pl.BlockSpec((tm, d), lambda i, j: (i, 0)),
            pl.BlockSpec((1, 1, d), lambda i, j: (i // per_batch, 0, 0)),
            pl.BlockSpec((1, 1, d), lambda i, j: (i // per_batch, 0, 0)),
            pl.BlockSpec((d, tn), lambda i, j: (0, j)),
        ],
        out_specs=pl.BlockSpec((tm, tn), lambda i, j: (i, j)),
        out_shape=jax.ShapeDtypeStruct((t, n), BF16),
        scratch_shapes=[pltpu.VMEM((tm, d), BF16)],
        compiler_params=_cparams(2),
        name="in_proj",
    )(x2, scale, shift, w_bf16)


def _sb_kernel(q_ref, k_ref, v_ref, g_ref, o_ref, *, tile):
    i = pl.program_id(2)
    n_groups = SB_QTILES
    qs = [q_ref[g * tile:(g + 1) * tile, :] for g in range(n_groups)]
    row = lax.broadcasted_iota(jnp.int32, (tile, tile), 0)
    col = lax.broadcasted_iota(jnp.int32, (tile, tile), 1)
    later_key = (row > col).astype(BF16)

    def sweep(jobs, carries):
        starts = [pl.multiple_of(blk * tile, tile) for _, blk, _ in jobs]
        zs = [lax.dot_general(qs[g], k_ref[pl.ds(ks, tile), :], (((1,), (1,)), ((), ())),
                              preferred_element_type=F32) for (g, _, _), ks in zip(jobs, starts)]
        softplus = []
        for z, (_, _, past) in zip(zs, jobs):
            sp = jnp.maximum(z, 0.0) + jnp.log(1.0 + jnp.exp(-jnp.abs(z)))
            softplus.append(sp if past is None else jnp.where(past, sp, 0.0))
        within = [jnp.dot(sp.astype(BF16), later_key, preferred_element_type=F32) for sp in softplus]
        carries = list(carries)
        weights = []
        for z, sp, inner, (g, _, past) in zip(zs, softplus, within, jobs):
            w = jnp.exp(z - sp - inner - carries[g])
            weights.append((w if past is None else jnp.where(past, w, 0.0)).astype(BF16))
            carries[g] = carries[g] + jnp.sum(sp, axis=1, keepdims=True)
        outs = [None] * n_groups
        for w, ks, (g, _, _) in zip(weights, starts, jobs):
            part = jnp.dot(w, v_ref[pl.ds(ks, tile), :], preferred_element_type=F32)
            outs[g] = part if outs[g] is None else outs[g] + part
        return tuple(carries), tuple(outs)

    def exists(blk):
        return jnp.broadcast_to(blk >= 0, (tile, tile))

    first = i * n_groups
    jobs = [(g, first + g, col < row) for g in range(n_groups)]
    jobs.append((0, jnp.maximum(first - 1, 0), exists(first - 1)))
    jobs += [(g, first + g - 1, None) for g in range(1, n_groups)]
    carries, accs = sweep(jobs, (jnp.zeros((tile, 1), F32),) * n_groups)

    def older(t, g):
        return first + g - 2 - t

    def cond(state):
        t, carries, _ = state
        more = False
        for g in range(n_groups):
            live = jnp.logical_and(older(t, g) >= 0, jnp.min(carries[g]) < SB_UNDERFLOW)
            more = jnp.logical_or(more, live)
        return more

    def body(state):
        t, carries, accs = state
        jobs = [(g, jnp.maximum(older(t, g), 0), exists(older(t, g))) for g in range(n_groups)]
        carries, outs = sweep(jobs, carries)
        return t + 1, carries, tuple(a + o for a, o in zip(accs, outs))

    _, _, accs = lax.while_loop(cond, body, (0, carries, accs))
    for g in range(n_groups):
        gate = g_ref[g * tile:(g + 1) * tile, :].astype(F32)
        o_ref[g * tile:(g + 1) * tile, :] = (accs[g] * (gate * jax.nn.sigmoid(gate))).astype(BF16)


def _sb_attention(proj, batch, seq):
    tile = SB_TILE
    rows = SB_QTILES * tile
    nq = seq // rows
    h = SB_HEADS
    kern = functools.partial(_sb_kernel, tile=tile)
    return pl.pallas_call(
        kern,
        grid=(batch, h, nq),
        in_specs=[
            pl.BlockSpec((rows, HEAD_DIM), lambda b, hh, i: (b * nq + i, hh)),
            pl.BlockSpec((seq, HEAD_DIM), lambda b, hh, i: (b, h + hh)),
            pl.BlockSpec((seq, HEAD_DIM), lambda b, hh, i: (b, 2 * h + hh)),
            pl.BlockSpec((rows, HEAD_DIM), lambda b, hh, i: (b * nq + i, 3 * h + hh)),
        ],
        out_specs=pl.BlockSpec((rows, HEAD_DIM), lambda b, hh, i: (b * nq + i, hh)),
        out_shape=jax.ShapeDtypeStruct((batch * seq, h * HEAD_DIM), BF16),
        compiler_params=_cparams(3),
        name="sb_attention",
    )(proj, proj, proj, proj)


def _s5_local_kernel(u_ref, mre_ref, mim_ref, sre_ref, sim_ref):
    u = u_ref[0]
    sre_ref[...] = jnp.dot(u, mre_ref[0], preferred_element_type=F32)
    sim_ref[...] = jnp.dot(u, mim_ref[0], preferred_element_type=F32)


def _s5_row_block(rows):
    return min(rows, 1024)


def _s5_local_state(u_flat, m_in_re, m_in_im):
    tiles, rows, flat = u_flat.shape
    p = m_in_re.shape[2]
    rb = _s5_row_block(rows)
    return pl.pallas_call(
        _s5_local_kernel,
        grid=(tiles, rows // rb),
        in_specs=[
            pl.BlockSpec((1, rb, flat), lambda g, r: (g, r, 0)),
            pl.BlockSpec((1, flat, p), lambda g, r: (g, 0, 0)),
            pl.BlockSpec((1, flat, p), lambda g, r: (g, 0, 0)),
        ],
        out_specs=[
            pl.BlockSpec((rb, p), lambda g, r: (r, g)),
            pl.BlockSpec((rb, p), lambda g, r: (r, g)),
        ],
        out_shape=[jax.ShapeDtypeStruct((rows, tiles * p), F32)] * 2,
        compiler_params=_cparams(2),
        name="s5_local_state",
    )(u_flat, m_in_re, m_in_im)


def _s5_scan_kernel(sre_ref, sim_ref, are_ref, aim_ref, ore_ref, oim_ref, *, batch):
    rows, width = sre_ref.shape
    per_tile = SUBLANES // batch
    a_re = are_ref[...]
    a_im = aim_ref[...]
    slot = lax.broadcasted_iota(jnp.int32, (SUBLANES, width), 0) // batch

    def spread(x, j):
        if per_tile == 1:
            return x
        return jnp.where(slot == j, x, pltpu.roll(x, batch, 0))

    def body(k, state):
        st_re, st_im = state
        tile_rows = pl.ds(pl.multiple_of(k * SUBLANES, SUBLANES), SUBLANES)
        x_re = sre_ref[tile_rows, :]
        x_im = sim_ref[tile_rows, :]
        out_re, out_im = st_re, st_im
        for j in range(per_tile):
            new_re = a_re * st_re - a_im * st_im + x_re
            new_im = a_re * st_im + a_im * st_re + x_im
            st_re, st_im = spread(new_re, j), spread(new_im, j)
            if j + 1 < per_tile:
                out_re = jnp.where(slot == j + 1, st_re, out_re)
                out_im = jnp.where(slot == j + 1, st_im, out_im)
        ore_ref[tile_rows, :] = out_re
        oim_ref[tile_rows, :] = out_im
        return st_re, st_im

    zero = jnp.zeros((SUBLANES, width), F32)
    lax.fori_loop(0, rows // SUBLANES, body, (zero, zero))


def _s5_scan(s_re, s_im, a_re, a_im, batch):
    rows, width = s_re.shape
    assert batch in (SUBLANES // 2, SUBLANES) and rows % SUBLANES == 0
    wt = 128
    blk = pl.BlockSpec((rows, wt), lambda j: (0, j))
    vec = pl.BlockSpec((1, wt), lambda j: (0, j))
    return pl.pallas_call(
        functools.partial(_s5_scan_kernel, batch=batch),
        grid=(width // wt,),
        in_specs=[blk, blk, vec, vec],
        out_specs=[blk, blk],
        out_shape=[jax.ShapeDtypeStruct((rows, width), F32)] * 2,
        compiler_params=_cparams(1),
        name="s5_scan",
    )(s_re, s_im, a_re, a_im)


def _s5_output_kernel(u_ref, t_ref, pre_ref, pim_ref, ore_ref, oim_ref, y_ref):
    y = jnp.dot(u_ref[0], t_ref[0], preferred_element_type=F32)
    y = y + jnp.dot(pre_ref[...].astype(BF16), ore_ref[0], preferred_element_type=F32)
    y = y + jnp.dot(pim_ref[...].astype(BF16), oim_ref[0], preferred_element_type=F32)
    y_ref[0] = y.astype(BF16)


def _s5_output(u_flat, toeplitz, prev_re, prev_im, m_out_re, m_out_im):
    tiles, rows, flat = u_flat.shape
    p = m_out_re.shape[1]
    rb = _s5_row_block(rows)
    return pl.pallas_call(
        _s5_output_kernel,
        grid=(tiles, rows // rb),
        in_specs=[
            pl.BlockSpec((1, rb, flat), lambda g, r: (g, r, 0)),
            pl.BlockSpec((1, flat, flat), lambda g, r: (g, 0, 0)),
            pl.BlockSpec((rb, p), lambda g, r: (r, g)),
            pl.BlockSpec((rb, p), lambda g, r: (r, g)),
            pl.BlockSpec((1, p, flat), lambda g, r: (g, 0, 0)),
            pl.BlockSpec((1, p, flat), lambda g, r: (g, 0, 0)),
        ],
        out_specs=pl.BlockSpec((1, rb, flat), lambda g, r: (g, r, 0)),
        out_shape=jax.ShapeDtypeStruct((tiles, rows, flat), BF16),
        compiler_params=_cparams(2),
        name="s5_output",
    )(u_flat, toeplitz, prev_re, prev_im, m_out_re, m_out_im)


def _s5_epilogue_kernel(y_ref, u_ref, g_ref, d_ref, w_ref, o_ref):
    y = y_ref[...].astype(F32) + d_ref[...] * u_ref[...].astype(F32)
    y = jax.nn.gelu(y)
    gate = jnp.dot(y.astype(BF16), w_ref[...], preferred_element_type=F32)
    y = y * jax.nn.sigmoid(gate)
    g = g_ref[...].astype(F32)
    o_ref[...] = (y * (g * jax.nn.sigmoid(g))).astype(BF16)


def _s5_epilogue(y, proj, d_skip, w_glu_bf16, u_col, g_col):
    t, w = y.shape
    tm = 1024
    return pl.pallas_call(
        _s5_epilogue_kernel,
        grid=(t // tm,),
        in_specs=[
            pl.BlockSpec((tm, w), lambda i: (i, 0)),
            pl.BlockSpec((tm, w), lambda i: (i, u_col)),
            pl.BlockSpec((tm, w), lambda i: (i, g_col)),
            pl.BlockSpec((1, w), lambda i: (0, 0)),
            pl.BlockSpec((w, w), lambda i: (0, 0)),
        ],
        out_specs=pl.BlockSpec((tm, w), lambda i: (i, 0)),
        out_shape=jax.ShapeDtypeStruct((t, w), BF16),
        compiler_params=_cparams(1),
        name="s5_epilogue",
    )(y, proj, proj, d_skip.reshape(1, w), w_glu_bf16)


def _s5_operators(a_re, a_im, log_dt, b_re, b_im, c_re, c_im):
    hi = lax.Precision.HIGHEST
    length = SSM_CHUNK
    lam_re, lam_im = a_re.astype(F32), a_im.astype(F32)
    dt = jnp.exp(log_dt.astype(F32))[:, None]
    steps = jnp.arange(length + 1, dtype=F32)[None, :, None]
    mag = jnp.exp((lam_re * dt)[:, None, :] * steps)
    ang = (lam_im * dt)[:, None, :] * steps
    pw_re, pw_im = mag * jnp.cos(ang), mag * jnp.sin(ang)
    num_re, num_im = pw_re[:, 1] - 1.0, pw_im[:, 1]
    den = lam_re * lam_re + lam_im * lam_im
    quo_re = (num_re * lam_re + num_im * lam_im) / den
    quo_im = (num_im * lam_re - num_re * lam_im) / den
    b_re, b_im = b_re.astype(F32), b_im.astype(F32)
    bb_re = quo_re[..., None] * b_re - quo_im[..., None] * b_im
    bb_im = quo_re[..., None] * b_im + quo_im[..., None] * b_re
    c_re, c_im = c_re.astype(F32)[:, None], c_im.astype(F32)[:, None]
    cp_re = c_re * pw_re[:, :, None, :] - c_im * pw_im[:, :, None, :]
    cp_im = c_re * pw_im[:, :, None, :] + c_im * pw_re[:, :, None, :]

    lag_kernel = (jnp.einsum('gnip,gpj->gjni', cp_re[:, :length], bb_re, precision=hi)
                  - jnp.einsum('gnip,gpj->gjni', cp_im[:, :length], bb_im, precision=hi)).astype(BF16)
    period = jnp.concatenate([lag_kernel, jnp.zeros_like(lag_kernel), jnp.zeros_like(lag_kernel[:, :, :1])],
                             axis=2)
    skew = jnp.tile(period, (1, 1, length, 1))[:, :, :2 * length * length]
    skew = skew.reshape(SSM_GROUPS, SSM_GROUP, length, 2 * length, SSM_GROUP)[:, :, :, :length]

    rev = length - 1 - jnp.arange(length)
    in_re = pw_re[:, rev, :, None] * bb_re[:, None] - pw_im[:, rev, :, None] * bb_im[:, None]
    in_im = pw_re[:, rev, :, None] * bb_im[:, None] + pw_im[:, rev, :, None] * bb_re[:, None]

    tg = SSM_TILE_GROUPS
    same_group = jnp.eye(tg, dtype=F32)

    def per_tile(m, perm, shape):
        m = m.astype(F32).reshape((SSM_TILES, tg) + m.shape[1:])
        m = m[..., None] * same_group.reshape((1, tg) + (1,) * (m.ndim - 2) + (tg,))
        return m.transpose(perm).reshape(shape).astype(BF16)

    return dict(
        toeplitz=per_tile(skew, (0, 3, 1, 2, 4, 6, 5), (SSM_TILES, SSM_FLAT, SSM_FLAT)),
        m_in_re=per_tile(in_re, (0, 2, 1, 4, 5, 3), (SSM_TILES, SSM_FLAT, tg * SSM_STATE)),
        m_in_im=per_tile(in_im, (0, 2, 1, 4, 5, 3), (SSM_TILES, SSM_FLAT, tg * SSM_STATE)),
        m_out_re=per_tile(cp_re[:, 1:], (0, 5, 4, 2, 1, 3), (SSM_TILES, tg * SSM_STATE, SSM_FLAT)),
        m_out_im=per_tile(-cp_im[:, 1:], (0, 5, 4, 2, 1, 3), (SSM_TILES, tg * SSM_STATE, SSM_FLAT)),
        a_re=pw_re[:, length].reshape(1, -1), a_im=pw_im[:, length].reshape(1, -1),
    )


def _s5_ssm(proj, batch, seq, ops, d_skip, w_glu_bf16, u_col0):
    t = batch * seq
    n_chunks = seq // SSM_CHUNK
    n_rows = batch * n_chunks
    lanes = SSM_WIDTH // SSM_TILES
    u = proj[:, u_col0:u_col0 + SSM_WIDTH]
    u_flat = (u.reshape(batch, n_chunks, SSM_CHUNK, SSM_TILES, lanes)
              .transpose(3, 1, 0, 2, 4).reshape(SSM_TILES, n_rows, SSM_FLAT))
    s_re, s_im = _s5_local_state(u_flat, ops['m_in_re'], ops['m_in_im'])
    prev_re, prev_im = _s5_scan(s_re, s_im, ops['a_re'], ops['a_im'], batch)
    y_flat = _s5_output(u_flat, ops['toeplitz'], prev_re, prev_im, ops['m_out_re'], ops['m_out_im'])
    y = (y_flat.reshape(SSM_TILES, n_chunks, batch, SSM_CHUNK, lanes)
         .transpose(2, 1, 3, 0, 4).reshape(t, SSM_WIDTH))
    return _s5_epilogue(y, proj, d_skip, w_glu_bf16, u_col0 // SSM_WIDTH, u_col0 // SSM_WIDTH + 1)


def _moba_kernel(q_ref, k_ref, v_ref, g_ref, o_ref, kaug_ref, kmean_ref, *, seq, q_scale):
    blk = MOBA_BLOCK
    nb = seq // blk
    i = pl.program_id(2)

    @pl.when(i == 0)
    def _():
        kmean_ref[...] = jnp.zeros_like(kmean_ref)
        block_lane = lax.broadcasted_iota(jnp.int32, (blk, HEAD_DIM), 1)

        def per_block(j, _):
            ks = pl.multiple_of(j * blk, blk)
            rows = k_ref[pl.ds(ks, blk), :]
            kaug_ref[pl.ds(ks, blk), :HEAD_DIM] = rows
            kaug_ref[pl.ds(ks, blk), HEAD_DIM:] = (block_lane == j).astype(BF16)
            kmean_ref[pl.ds(j, 1), :] = jnp.sum(rows.astype(F32), axis=0, keepdims=True) * (1.0 / blk)
            return 0

        lax.fori_loop(0, nb, per_block, 0)

    tq = MOBA_QBLOCKS * blk
    first_qb = i * MOBA_QBLOCKS
    q = q_ref[...]
    kmean = kmean_ref[...]
    km_hi = kmean.astype(BF16)
    km_lo = (kmean - km_hi.astype(F32)).astype(BF16)
    contract_last = (((1,), (1,)), ((), ()))

    gate_t = (lax.dot_general(km_hi, q, contract_last, preferred_element_type=F32)
              + lax.dot_general(km_lo, q, contract_last, preferred_element_type=F32))[:nb]
    key_blk = lax.broadcasted_iota(jnp.int32, (nb, tq), 0)
    q_col = lax.broadcasted_iota(jnp.int32, (nb, tq), 1)
    q_block_t = first_qb
    for extra in range(1, MOBA_QBLOCKS):
        q_block_t = q_block_t + (q_col >= extra * blk).astype(jnp.int32)
    key_blk_f = key_blk.astype(F32)
    valid = jnp.logical_and(key_blk < q_block_t, gate_t > 0.5 * NEG * q_scale)
    live = jnp.where(valid, gate_t, -jnp.inf)
    selected = jnp.zeros((nb, tq), jnp.bool_)
    for _ in range(MOBA_TOPK):
        top = jnp.max(live, axis=0, keepdims=True)
        hit = jnp.logical_and(live == top, top > -jnp.inf)
        first = jnp.min(jnp.where(hit, key_blk_f, 1e9), axis=0, keepdims=True)
        pick = key_blk_f == first
        selected = jnp.logical_or(selected, pick)
        live = jnp.where(pick, -jnp.inf, live)
    sel_t = jnp.concatenate([selected.astype(F32), jnp.zeros((HEAD_DIM - nb, tq), F32)], axis=0)
    sel = sel_t.T > 0.5
    lane = lax.broadcasted_iota(jnp.int32, (tq, HEAD_DIM), 1)
    row = lax.broadcasted_iota(jnp.int32, (tq, HEAD_DIM), 0)
    q_block = first_qb
    for extra in range(1, MOBA_QBLOCKS):
        q_block = q_block + (row >= extra * blk).astype(jnp.int32)
    bias_past = jnp.where(jnp.logical_and(sel, lane < first_qb), 0.0, NEG).astype(BF16)
    bias_own = jnp.where(jnp.logical_or(sel, lane == q_block), 0.0, NEG).astype(BF16)
    q_past = jnp.concatenate([q, bias_past], axis=1)
    q_own = jnp.concatenate([q, bias_own], axis=1)

    rows_per = MOBA_ROWS
    n_groups = tq // rows_per
    own = pl.multiple_of(first_qb * blk, blk)
    scores = []
    for r in range(n_groups):
        n_own = ((r * rows_per) // blk + 1) * blk
        s = lax.dot_general(q_own[r * rows_per:(r + 1) * rows_per], kaug_ref[pl.ds(own, n_own), :],
                            contract_last, preferred_element_type=F32)
        q_row = lax.broadcasted_iota(jnp.int32, (rows_per, n_own), 0) + r * rows_per
        k_col = lax.broadcasted_iota(jnp.int32, (rows_per, n_own), 1)
        scores.append(jnp.where(k_col <= q_row, s, -jnp.inf))
    soft = []
    for s in scores:
        m = jnp.max(s, axis=1, keepdims=True)
        p = jnp.exp2(s - m)
        soft.append((m, jnp.sum(p, axis=1, keepdims=True), p.astype(BF16)))
    state = []
    for m, l, p in soft:
        acc = jnp.dot(p, v_ref[pl.ds(own, p.shape[1]), :], preferred_element_type=F32)
        state.append((m, l, acc))

    chunk = MOBA_KCHUNK * blk

    def body(c, state):
        ks = pl.multiple_of(c * chunk, chunk)
        k_chunk = kaug_ref[pl.ds(ks, chunk), :]
        v_chunk = v_ref[pl.ds(ks, chunk), :]
        scores = [lax.dot_general(q_past[r * rows_per:(r + 1) * rows_per], k_chunk, contract_last,
                                  preferred_element_type=F32) for r in range(n_groups)]
        soft = []
        for (m, l, acc), s in zip(state, scores):
            m_new = jnp.maximum(m, jnp.max(s, axis=1, keepdims=True))
            alpha = jnp.exp2(m - m_new)
            p = jnp.exp2(s - m_new)
            soft.append((m_new, alpha * l + jnp.sum(p, axis=1, keepdims=True), alpha * acc,
                         p.astype(BF16)))
        return tuple((m, l, acc + jnp.dot(p, v_chunk, preferred_element_type=F32))
                     for m, l, acc, p in soft)

    n_chunks = (first_qb + MOBA_KCHUNK - 1) // MOBA_KCHUNK
    state = lax.fori_loop(0, n_chunks, body, tuple(state))
    for r in range(tq // rows_per):
        _, l, acc = state[r]
        rows = slice(r * rows_per, (r + 1) * rows_per)
        g = g_ref[rows, :].astype(F32)
        o_ref[rows, :] = (acc / l * (g * jax.nn.sigmoid(g))).astype(BF16)


def _moba_attention(proj, batch, seq, q_scale):
    blk = MOBA_QBLOCKS * MOBA_BLOCK
    n_blocks = seq // MOBA_BLOCK
    assert seq % (MOBA_KCHUNK * MOBA_BLOCK) == 0 and n_blocks % 8 == 0 and n_blocks <= HEAD_DIM
    nq = seq // blk
    h = MOBA_HEADS
    kern = functools.partial(_moba_kernel, seq=seq, q_scale=q_scale)
    return pl.pallas_call(
        kern,
        grid=(batch, h, nq),
        in_specs=[
            pl.BlockSpec((blk, HEAD_DIM), lambda b, hh, i: (b * nq + i, hh)),
            pl.BlockSpec((seq, HEAD_DIM), lambda b, hh, i: (b, h + hh)),
            pl.BlockSpec((seq, HEAD_DIM), lambda b, hh, i: (b, 2 * h + hh)),
            pl.BlockSpec((blk, HEAD_DIM), lambda b, hh, i: (b * nq + i, 3 * h + hh)),
        ],
        out_specs=pl.BlockSpec((blk, HEAD_DIM), lambda b, hh, i: (b * nq + i, hh)),
        out_shape=jax.ShapeDtypeStruct((batch * seq, h * HEAD_DIM), BF16),
        scratch_shapes=[pltpu.VMEM((seq, 2 * HEAD_DIM), BF16), pltpu.VMEM((HEAD_DIM, HEAD_DIM), F32)],
        compiler_params=_cparams(3),
        name="moba_attention",
    )(proj, proj, proj, proj)


def _out_proj_kernel(*refs, n_in, alpha):
    a_refs = refs[:n_in]
    w_refs = refs[n_in:2 * n_in]
    x_ref, gate_ref, lng_ref, lnb_ref, o_ref = refs[2 * n_in:]
    y = jnp.dot(a_refs[0][...], w_refs[0][...], preferred_element_type=F32)
    for a_ref, w_ref in zip(a_refs[1:], w_refs[1:]):
        y = y + jnp.dot(a_ref[...], w_ref[...], preferred_element_type=F32)
    z = alpha * x_ref[...] + (1.0 + gate_ref[0]) * y
    mu = jnp.mean(z, axis=-1, keepdims=True)
    zc = z - mu
    var = jnp.mean(zc * zc, axis=-1, keepdims=True)
    o_ref[...] = zc * lax.rsqrt(var + LN_EPS) * lng_ref[...] + lnb_ref[...]


def _out_proj_ln(acts, weights, x2, gate, ln_g, ln_b, seq, alpha):
    t, d = x2.shape
    tm = 512
    per_batch = seq // tm
    n_in = len(acts)
    kern = functools.partial(_out_proj_kernel, n_in=n_in, alpha=alpha)
    in_specs = ([pl.BlockSpec((tm, a.shape[1]), lambda i: (i, 0)) for a in acts]
                + [pl.BlockSpec(w.shape, lambda i: (0, 0)) for w in weights]
                + [pl.BlockSpec((tm, d), lambda i: (i, 0)),
                   pl.BlockSpec((1, 1, d), lambda i: (i // per_batch, 0, 0)),
                   pl.BlockSpec((1, d), lambda i: (0, 0)),
                   pl.BlockSpec((1, d), lambda i: (0, 0))])
    return pl.pallas_call(
        kern,
        grid=(t // tm,),
        in_specs=in_specs,
        out_specs=pl.BlockSpec((tm, d), lambda i: (i, 0)),
        out_shape=jax.ShapeDtypeStruct((t, d), F32),
        compiler_params=_cparams(1),
        name="out_proj_ln",
    )(*acts, *weights, x2, gate, ln_g.reshape(1, d), ln_b.reshape(1, d))


def kernel(x, c, ada_w, ada_b, ln_g, ln_b, even_w_in, even_w_out, ssm_a_re, ssm_a_im, ssm_log_dt,
           ssm_b_re, ssm_b_im, ssm_c_re, ssm_c_im, ssm_d, ssm_w_glu, odd_w_in, odd_w_out):
    batch, seq, d = x.shape
    depth = ada_w.shape[0]
    alpha = (2 * depth) ** 0.25
    q_scale = HEAD_DIM ** -0.5
    sb_width = SB_HEADS * HEAD_DIM

    c_pad = jnp.pad(c, ((0, 8 - batch), (0, 0)))
    mods = _ada_mod(c_pad, ada_w, ada_b)[:, :batch]
    x2 = x.reshape(batch * seq, d)
    for layer in range(depth):
        shift = mods[layer, :, None, :d]
        scale = mods[layer, :, None, d:2 * d]
        gate = mods[layer, :, None, 2 * d:]
        idx = layer // 2
        if layer % 2 == 0:
            proj = _in_proj(x2, scale, shift, even_w_in[idx].astype(BF16), seq,
                            q_width=sb_width, q_scale=q_scale)
            o_sb = _sb_attention(proj, batch, seq)
            ops = _s5_operators(ssm_a_re[idx], ssm_a_im[idx], ssm_log_dt[idx], ssm_b_re[idx],
                                ssm_b_im[idx], ssm_c_re[idx], ssm_c_im[idx])
            o_ssm = _s5_ssm(proj, batch, seq, ops, ssm_d[idx], ssm_w_glu[idx].astype(BF16),
                            4 * sb_width)
            w_out = even_w_out[idx].astype(BF16)
            acts, weights = [o_sb, o_ssm], [w_out[:sb_width], w_out[sb_width:]]
        else:
            q_scale2 = q_scale * math.log2(math.e)
            proj = _in_proj(x2, scale, shift, odd_w_in[idx].astype(BF16), seq,
                            q_width=MOBA_HEADS * HEAD_DIM, q_scale=q_scale2)
            o = _moba_attention(proj, batch, seq, q_scale2)
            acts, weights = [o], [odd_w_out[idx].astype(BF16)]
        x2 = _out_proj_ln(acts, weights, x2, gate, ln_g[layer], ln_b[layer], seq, alpha)
    return x2.reshape(batch, seq, d)
```
